```python
import math
import jax, jax.numpy as jnp
from jax import lax
import numpy as np

D_MODEL = 1024
BATCH = 4
SEQ = 4096
DEPTH = 4

N_A_LAYERS = DEPTH // 2
N_B_LAYERS = DEPTH - N_A_LAYERS
CHUNK = 128
SGU_WIDTH = 2 * D_MODEL
SGU_GROUPS = 8
HEAD_DIM = 64
KV_HEADS = D_MODEL // 128
DILATED_GROUPS = ((128, 1), (512, 4), (2048, 16))
N_GROUPS = len(DILATED_GROUPS)
Q_HEADS = N_GROUPS * KV_HEADS
BLK = 128
REL_BUCKETS = 32
REL_MAX_DIST = 2048
D_FF = 2816
CONV_WIDTH = 3
ALPHA = (2 * DEPTH) ** 0.25
BETA = (8 * DEPTH) ** -0.25
LN_EPS = 1e-5
NEG = -1e30

kernel_name = 'hybrid_sgu_dilated_yoco'


def layer_norm(x, g, b):
    xf = x.astype(jnp.float32)
    mu = jnp.mean(xf, axis=-1, keepdims=True)
    var = jnp.mean(jnp.square(xf - mu), axis=-1, keepdims=True)
    y = (xf - mu) * lax.rsqrt(var + LN_EPS) * g.astype(jnp.float32) + b.astype(jnp.float32)
    return y.astype(x.dtype)


def conv_ffn(x, w_up, conv_w, conv_b, w_down):
    T = x.shape[1]
    h = x @ w_up
    hp = jnp.pad(h, ((0, 0), (CONV_WIDTH - 1, 0), (0, 0)))
    h = sum(hp[:, k:k + T] * conv_w[k] for k in range(CONV_WIDTH)) + conv_b
    a, g = jnp.split(h, 2, axis=-1)
    return (jax.nn.gelu(a, approximate=False) * g) @ w_down


def mixer_a(x, w_in, ln_g, ln_b, w_s, b_s, w_out):
    B, T, _ = x.shape
    z = jax.nn.gelu(x @ w_in, approximate=False)
    u, v = jnp.split(z, 2, axis=-1)
    v = layer_norm(v, ln_g, ln_b)
    v = v.reshape(B, T // CHUNK, CHUNK, SGU_GROUPS, SGU_WIDTH // SGU_GROUPS)
    ws = w_s * jnp.tril(jnp.ones((CHUNK, CHUNK), w_s.dtype))
    sv = jnp.einsum('gts,bnsgc->bntgc', ws, v) + b_s.T[None, None, :, :, None]
    y = u * sv.reshape(B, T, SGU_WIDTH)
    return y @ w_out


def t5_bucket(dist):
    n = dist.astype(jnp.int32)
    max_exact = REL_BUCKETS // 2
    nf = jnp.maximum(n, 1).astype(jnp.float32)
    large = max_exact + (jnp.log(nf / max_exact) / math.log(REL_MAX_DIST / max_exact)
                         * (REL_BUCKETS - max_exact)).astype(jnp.int32)
    large = jnp.minimum(large, REL_BUCKETS - 1)
    return jnp.where(n < max_exact, n, large)


def band_delta():
    iq = jnp.arange(BLK)[:, None]
    ik = jnp.arange(2 * BLK)[None, :]
    return iq + BLK - ik


def band_bias(table_g, dil):
    dist = jnp.clip(band_delta(), 0, None) * dil
    return jnp.transpose(table_g[t5_bucket(dist)], (2, 0, 1)).astype(jnp.float32)


def dilated_band_attention(q, k, v, dil, n_back, bias):
    B, T, H, Dh = q.shape
    span = dil * BLK
    Tp = -(-T // span) * span
    S = Tp // dil
    NB = S // BLK

    def to_blocks(a):
        a = jnp.pad(a, ((0, 0), (0, Tp - T), (0, 0), (0, 0)))
        a = a.reshape(B, S, dil, H, Dh).transpose(0, 2, 3, 1, 4)
        return a.reshape(B, dil, H, NB, BLK, Dh)

    def band(a):
        prev = jnp.pad(a[:, :, :, :-1], ((0, 0), (0, 0), (0, 0), (1, 0), (0, 0), (0, 0)))
        return jnp.concatenate([prev, a], axis=4)

    qb, kb, vb = to_blocks(q), band(to_blocks(k)), band(to_blocks(v))
    s = jnp.einsum('brhnqd,brhnkd->brhnqk', qb, kb).astype(jnp.float32) * (Dh ** -0.5)
    s = s + bias[:, None]
    delta = band_delta()
    valid = (delta >= 0) & (delta <= n_back)
    first = (jnp.arange(NB)[:, None, None] == 0) & (jnp.arange(2 * BLK)[None, None, :] < BLK)
    valid = valid[None] & ~first
    s = jnp.where(valid, s, NEG)
    m = jnp.max(s, axis=-1, keepdims=True)
    p = jnp.exp(s - m)
    den = jnp.sum(p, axis=-1, keepdims=True)
    o = jnp.einsum('brhnqk,brhnkd->brhnqd', (p / den).astype(v.dtype), vb)
    lse = (m + jnp.log(den))[..., 0]
    o = o.reshape(B, dil, H, S, Dh).transpose(0, 3, 1, 2, 4).reshape(B, Tp, H, Dh)[:, :T]
    lse = lse.reshape(B, dil, H, S).transpose(0, 3, 1, 2).reshape(B, Tp, H)[:, :T]
    return o, lse


def mixer_b(x, k, v, w_q, w_o, rel_table):
    B, T, _ = x.shape
    q = (x @ w_q).reshape(B, T, N_GROUPS, KV_HEADS, HEAD_DIM)
    outs, lses = [], []
    for g, (win, dil) in enumerate(DILATED_GROUPS):
        bias = band_bias(rel_table[:, g * KV_HEADS:(g + 1) * KV_HEADS], dil)
        o, lse = dilated_band_attention(q[:, :, g], k, v, dil, win // dil, bias)
        outs.append(o)
        lses.append(lse)
    wts = jax.nn.softmax(jnp.stack(lses, axis=0), axis=0)
    o = jnp.einsum('gbth,gbthd->bthd', wts, jnp.stack(outs, axis=0).astype(jnp.float32))
    return o.reshape(B, T, KV_HEADS * HEAD_DIM).astype(x.dtype) @ w_o


def setup_inputs(seed: int = 0) -> dict:
    key = jax.random.key(seed)
    ks = jax.random.split(key, 20)
    f32 = jnp.float32
    nrm = lambda k, shape, s: jax.random.normal(k, shape, f32) * s
    E2 = 2 * SGU_WIDTH
    return {
        'x': nrm(ks[0], (BATCH, SEQ, D_MODEL), 1.0),
        'a_w_in': nrm(ks[1], (N_A_LAYERS, D_MODEL, E2), D_MODEL ** -0.5),
        'a_ln_g': 1.0 + nrm(ks[2], (N_A_LAYERS, SGU_WIDTH), 0.05),
        'a_ln_b': nrm(ks[3], (N_A_LAYERS, SGU_WIDTH), 0.02),
        'a_w_s': nrm(ks[4], (N_A_LAYERS, SGU_GROUPS, CHUNK, CHUNK), CHUNK ** -0.5),
        'a_b_s': 1.0 + nrm(ks[5], (N_A_LAYERS, SGU_GROUPS, CHUNK), 0.1),
        'a_w_out': nrm(ks[6], (N_A_LAYERS, SGU_WIDTH, D_MODEL), BETA * SGU_WIDTH ** -0.5),
        'kv_w': nrm(ks[7], (D_MODEL, 2 * KV_HEADS * HEAD_DIM), D_MODEL ** -0.5),
        'b_w_q': nrm(ks[8], (N_B_LAYERS, D_MODEL, Q_HEADS * HEAD_DIM), D_MODEL ** -0.5),
        'b_w_o': nrm(ks[9], (N_B_LAYERS, KV_HEADS * HEAD_DIM, D_MODEL), BETA * (KV_HEADS * HEAD_DIM) ** -0.5),
        'rel_table': nrm(ks[10], (REL_BUCKETS, Q_HEADS), 0.5),
        'ffn_w_up': nrm(ks[11], (DEPTH, D_MODEL, 2 * D_FF), D_MODEL ** -0.5),
        'ffn_conv_w': nrm(ks[12], (DEPTH, CONV_WIDTH, 2 * D_FF), CONV_WIDTH ** -0.5),
        'ffn_conv_b': nrm(ks[13], (DEPTH, 2 * D_FF), 0.02),
        'ffn_w_down': nrm(ks[14], (DEPTH, D_FF, D_MODEL), BETA * D_FF ** -0.5),
        'ln_g': 1.0 + nrm(ks[15], (DEPTH, 2, D_MODEL), 0.05),
        'ln_b': nrm(ks[16], (DEPTH, 2, D_MODEL), 0.02),
    }


def reference(x, a_w_in, a_ln_g, a_ln_b, a_w_s, a_b_s, a_w_out, kv_w, b_w_q, b_w_o,
              rel_table, ffn_w_up, ffn_conv_w, ffn_conv_b, ffn_w_down, ln_g, ln_b):
    B, T, _ = x.shape
    k = v = None
    for i in range(DEPTH):
        if i < N_A_LAYERS:
            h = mixer_a(x, a_w_in[i], a_ln_g[i], a_ln_b[i], a_w_s[i], a_b_s[i], a_w_out[i])
        else:
            if i == N_A_LAYERS:
                kv = (x @ kv_w).reshape(B, T, 2, KV_HEADS, HEAD_DIM)
                k, v = kv[:, :, 0], kv[:, :, 1]
            j = i - N_A_LAYERS
            h = mixer_b(x, k, v, b_w_q[j], b_w_o[j], rel_table)
        x = layer_norm(ALPHA * x + h, ln_g[i, 0], ln_b[i, 0])
        f = conv_ffn(x, ffn_w_up[i], ffn_conv_w[i], ffn_conv_b[i], ffn_w_down[i])
        x = layer_norm(ALPHA * x + f, ln_g[i, 1], ln_b[i, 1])
    return x
```

```python
import functools
import math

import jax
import jax.numpy as jnp
from jax import lax
from jax.experimental import pallas as pl
from jax.experimental.pallas import tpu as pltpu

D_MODEL = 1024
DEPTH = 4
N_A_LAYERS = DEPTH // 2
CHUNK = 128
SGU_WIDTH = 2 * D_MODEL
SGU_GROUPS = 8
SGU_GROUP_WIDTH = SGU_WIDTH // SGU_GROUPS
HEAD_DIM = 64
KV_HEADS = D_MODEL // 128
KV_WIDTH = KV_HEADS * HEAD_DIM
DILATED_GROUPS = ((128, 1), (512, 4), (2048, 16))
N_GROUPS = len(DILATED_GROUPS)
BLK = 128
REL_BUCKETS = 32
REL_MAX_DIST = 2048
D_FF = 2816
CONV_WIDTH = 3
ALPHA = (2 * DEPTH) ** 0.25
LN_EPS = 1e-5
NEG = -1e30

F32 = jnp.float32
BF16 = jnp.bfloat16

V7X_SUBLANES = 8
V7X_LANES = 128
V7X_MXU_DIM = 256
V7X_VMEM_LIMIT_BYTES = 56 * 1024 * 1024

ROW_TILE = 512
FF_BLOCK = V7X_MXU_DIM
N_FF_BLOCKS = D_FF // FF_BLOCK
SPAN_TILE = max(w for w, _ in DILATED_GROUPS)
HEAD_SPLIT = 2
HEAD_COLS = KV_WIDTH // HEAD_SPLIT
HEADS_PER_STEP = KV_HEADS // HEAD_SPLIT
HEADS_PER_LANE_TILE = V7X_LANES // HEAD_DIM

assert D_FF % FF_BLOCK == 0
assert all(w // d == BLK for w, d in DILATED_GROUPS)


def _gelu(x):
    return 0.5 * x * (1.0 + lax.erf(x * (1.0 / math.sqrt(2.0))))


def _layer_norm(x, g, b):
    mu = jnp.mean(x, axis=-1, keepdims=True)
    xc = x - mu
    var = jnp.mean(xc * xc, axis=-1, keepdims=True)
    return xc * lax.rsqrt(var + LN_EPS) * g + b


def _params(*semantics):
    return pltpu.CompilerParams(dimension_semantics=semantics,
                                vmem_limit_bytes=V7X_VMEM_LIMIT_BYTES)


def _resident(shape):
    zeros = (0,) * len(shape)
    return pl.BlockSpec(shape, lambda *_: zeros, pipeline_mode=pl.Buffered(1))


def _sgu_kernel(x_ref, win_ref, lng_ref, lnb_ref, ws_ref, bs_ref, wout_ref,
                g_ref, b_ref, o_ref, vn_ref, acc_ref):
    tm = x_ref.shape[0]
    x = x_ref[...]
    xb = x.astype(BF16)
    v = _gelu(jnp.dot(xb, win_ref[:, SGU_WIDTH:], preferred_element_type=F32))
    vn_ref[...] = _layer_norm(v, lng_ref[...], lnb_ref[...]).astype(BF16)

    row = lax.broadcasted_iota(jnp.int32, (CHUNK, CHUNK), 0)
    col = lax.broadcasted_iota(jnp.int32, (CHUNK, CHUNK), 1)
    causal = row >= col
    bs = bs_ref[...]
    acc_ref[...] = jnp.zeros_like(acc_ref)
    for g in range(SGU_GROUPS):
        cols = slice(g * SGU_GROUP_WIDTH, (g + 1) * SGU_GROUP_WIDTH)
        ws = jnp.where(causal, ws_ref[g], 0.0).astype(BF16)
        u = _gelu(jnp.dot(xb, win_ref[:, cols], preferred_element_type=F32))
        sv = [jnp.dot(ws, vn_ref[c * CHUNK:(c + 1) * CHUNK, cols],
                      preferred_element_type=F32) + bs[:, g:g + 1]
              for c in range(tm // CHUNK)]
        y = u * jnp.concatenate(sv, axis=0)
        acc_ref[...] += jnp.dot(y.astype(BF16), wout_ref[cols, :],
                                preferred_element_type=F32)
    o_ref[...] = _layer_norm(ALPHA * x + acc_ref[...], g_ref[...], b_ref[...])


def _mixer_a_layer(x2, w_in, ln_g, ln_b, w_s, b_s, w_out, g, b):
    n = x2.shape[0]
    tm = ROW_TILE
    return pl.pallas_call(
        _sgu_kernel,
        grid=(n // tm,),
        in_specs=[
            pl.BlockSpec((tm, D_MODEL), lambda i: (i, 0)),
            _resident((D_MODEL, 2 * SGU_WIDTH)),
            _resident((1, SGU_WIDTH)),
            _resident((1, SGU_WIDTH)),
            _resident((SGU_GROUPS, CHUNK, CHUNK)),
            _resident((CHUNK, SGU_GROUPS)),
            _resident((SGU_WIDTH, D_MODEL)),
            _resident((1, D_MODEL)),
            _resident((1, D_MODEL)),
        ],
        out_specs=pl.BlockSpec((tm, D_MODEL), lambda i: (i, 0)),
        out_shape=jax.ShapeDtypeStruct((n, D_MODEL), F32),
        scratch_shapes=[pltpu.VMEM((tm, SGU_WIDTH), BF16),
                        pltpu.VMEM((tm, D_MODEL), F32)],
        compiler_params=_params("parallel"),
        name="sgu_mixer",
    )(x2, w_in.astype(BF16), ln_g.reshape(1, -1), ln_b.reshape(1, -1), w_s,
      b_s.T, w_out.astype(BF16), g.reshape(1, -1), b.reshape(1, -1))


def _ffn_kernel(x_ref, wup_ref, cw_ref, cb_ref, wdn_ref, g_ref, b_ref, o_ref,
                hist_ref, carry_ref, acc_ref, *, tiles_per_seq):
    tm = x_ref.shape[0]
    pad = V7X_SUBLANES
    x = x_ref[...]
    xb = x.astype(BF16)
    seq_start = (pl.program_id(0) % tiles_per_seq) == 0
    acc_ref[...] = jnp.zeros_like(acc_ref)

    def conv_half(blk, slot):
        h = jnp.dot(xb, wup_ref[blk], preferred_element_type=F32)
        hist_ref[slot, 0:pad, :] = jnp.where(seq_start, 0.0, carry_ref[blk])
        hist_ref[slot, pad:pad + tm, :] = h
        carry_ref[blk] = h[tm - pad:tm]
        cw = cw_ref[blk]
        return (hist_ref[slot, pad - 2:pad - 2 + tm, :] * cw[0:1]
                + hist_ref[slot, pad - 1:pad - 1 + tm, :] * cw[1:2]
                + h * cw[2:3] + cb_ref[blk])

    def body(j, carry):
        a = conv_half(j, 0)
        gate = conv_half(N_FF_BLOCKS + j, 1)
        act = (_gelu(a) * gate).astype(BF16)
        acc_ref[...] += jnp.dot(act, wdn_ref[j], preferred_element_type=F32)
        return carry

    lax.fori_loop(0, N_FF_BLOCKS, body, 0)
    o_ref[...] = _layer_norm(ALPHA * x + acc_ref[...], g_ref[...], b_ref[...])


def _conv_ffn_layer(x2, seq_len, w_up, conv_w, conv_b, w_down, g, b):
    n = x2.shape[0]
    tm = ROW_TILE
    nb2 = 2 * N_FF_BLOCKS
    w_up_b = w_up.astype(BF16).reshape(D_MODEL, nb2, FF_BLOCK).transpose(1, 0, 2)
    cw_b = conv_w.reshape(CONV_WIDTH, nb2, FF_BLOCK).transpose(1, 0, 2)
    cb_b = conv_b.reshape(nb2, 1, FF_BLOCK)
    w_dn_b = w_down.astype(BF16).reshape(N_FF_BLOCKS, FF_BLOCK, D_MODEL)
    return pl.pallas_call(
        functools.partial(_ffn_kernel, tiles_per_seq=seq_len // tm),
        grid=(n // tm,),
        in_specs=[
            pl.BlockSpec((tm, D_MODEL), lambda i: (i, 0)),
            _resident((nb2, D_MODEL, FF_BLOCK)),
            _resident((nb2, CONV_WIDTH, FF_BLOCK)),
            _resident((nb2, 1, FF_BLOCK)),
            _resident((N_FF_BLOCKS, FF_BLOCK, D_MODEL)),
            _resident((1, D_MODEL)),
            _resident((1, D_MODEL)),
        ],
        out_specs=pl.BlockSpec((tm, D_MODEL), lambda i: (i, 0)),
        out_shape=jax.ShapeDtypeStruct((n, D_MODEL), F32),
        scratch_shapes=[
            pltpu.VMEM((2, tm + V7X_SUBLANES, FF_BLOCK), F32),
            pltpu.VMEM((nb2, V7X_SUBLANES, FF_BLOCK), F32),
            pltpu.VMEM((tm, D_MODEL), F32),
        ],
        compiler_params=_params("arbitrary"),
        name="conv_ffn",
    )(x2, w_up_b, cw_b, cb_b, w_dn_b, g.reshape(1, -1), b.reshape(1, -1))


def _regroup_proj_kernel(x_ref, w_ref, *rest, plan, scale):
    n_out = len({o for routes in plan for (o, _, _) in routes})
    out_refs, res_ref = rest[:n_out], rest[n_out]
    tm = x_ref.shape[0]
    cb = V7X_MXU_DIM
    xb = x_ref[...].astype(BF16)
    for c, routes in enumerate(plan):
        res = jnp.dot(xb, w_ref[:, c * cb:(c + 1) * cb], preferred_element_type=F32)
        if scale != 1.0:
            res = res * scale
        for l in range(cb // V7X_LANES):
            res_ref[l] = res[:, l * V7X_LANES:(l + 1) * V7X_LANES]
        for (o, oc, d) in routes:
            out = out_refs[o]
            if d == 1:
                out[:, oc * cb:(oc + 1) * cb] = res.astype(out.dtype)
                continue
            span = BLK * d
            for s in range(tm // span):
                for r in range(d):
                    rows = jnp.concatenate(
                        [res_ref[l, pl.ds(s * span + r, BLK, stride=d), :]
                         for l in range(cb // V7X_LANES)], axis=-1)
                    out[s * span + r * BLK:s * span + (r + 1) * BLK,
                        oc * cb:(oc + 1) * cb] = rows.astype(out.dtype)


def _regroup_proj(x2, w, plan, out_widths, scale=1.0):
    n = x2.shape[0]
    tm = SPAN_TILE
    return pl.pallas_call(
        functools.partial(_regroup_proj_kernel, plan=plan, scale=scale),
        grid=(n // tm,),
        in_specs=[pl.BlockSpec((tm, D_MODEL), lambda i: (i, 0)),
                  _resident(w.shape)],
        out_specs=[pl.BlockSpec((tm, wd), lambda i: (i, 0)) for wd in out_widths],
        out_shape=[jax.ShapeDtypeStruct((n, wd), BF16) for wd in out_widths],
        scratch_shapes=[pltpu.VMEM((V7X_MXU_DIM // V7X_LANES, tm, V7X_LANES), F32)],
        compiler_params=_params("parallel"),
        name="regroup_proj",
    )(x2, w.astype(BF16))


def _bias_kernel(table_ref, idx_ref, o_ref):
    g = pl.program_id(0)
    idx = idx_ref[0]
    for h in range(KV_HEADS):
        acc = jnp.zeros(idx.shape, F32)
        for bkt in range(REL_BUCKETS):
            acc = jnp.where(idx == bkt,
                            table_ref[bkt * (N_GROUPS * KV_HEADS) + g * KV_HEADS + h], acc)
        o_ref[0, h] = acc


def _t5_bucket(dist):
    n = dist.astype(jnp.int32)
    max_exact = REL_BUCKETS // 2
    nf = jnp.maximum(n, 1).astype(F32)
    large = max_exact + (jnp.log(nf / max_exact) / math.log(REL_MAX_DIST / max_exact)
                         * (REL_BUCKETS - max_exact)).astype(jnp.int32)
    large = jnp.minimum(large, REL_BUCKETS - 1)
    return jnp.where(n < max_exact, n, large)


def _band_bias(rel_table):
    iq = jnp.arange(BLK)[:, None]
    ik = jnp.arange(2 * BLK)[None, :]
    delta = jnp.clip(iq + BLK - ik, 0, None)
    idx = jnp.stack([_t5_bucket(delta * d) for _, d in DILATED_GROUPS])
    return pl.pallas_call(
        _bias_kernel,
        grid=(N_GROUPS,),
        in_specs=[pl.BlockSpec(memory_space=pltpu.SMEM),
                  pl.BlockSpec((1, BLK, 2 * BLK), lambda g: (g, 0, 0))],
        out_specs=pl.BlockSpec((1, KV_HEADS, BLK, 2 * BLK), lambda g: (g, 0, 0, 0)),
        out_shape=jax.ShapeDtypeStruct((N_GROUPS, KV_HEADS, BLK, 2 * BLK), F32),
        compiler_params=_params("parallel"),
        name="band_bias",
    )(rel_table.reshape(-1), idx)


def _attn_kernel(bias_ref, *refs, tiles_per_seq):
    q_refs = refs[0:N_GROUPS]
    kv_refs = refs[N_GROUPS:5 * N_GROUPS]
    o_ref = refs[5 * N_GROUPS]
    og_ref, lse_ref = refs[5 * N_GROUPS + 1:]
    tm = o_ref.shape[0]
    n_blocks = tm // BLK
    seq_start = (pl.program_id(0) % tiles_per_seq) == 0

    row = lax.broadcasted_iota(jnp.int32, (BLK, BLK), 0)
    col = lax.broadcasted_iota(jnp.int32, (BLK, BLK), 1)
    keep_prev = col >= row
    keep_cur = col <= row

    for g, (_, d) in enumerate(DILATED_GROUPS):
        q_ref = q_refs[g]
        kc_ref, vc_ref, kp_ref, vp_ref = kv_refs[4 * g:4 * g + 4]

        def block(blk, carry, g=g, d=d, q_ref=q_ref, kc_ref=kc_ref, vc_ref=vc_ref,
                  kp_ref=kp_ref, vp_ref=vp_ref):
            span = blk // d
            res = blk % d
            first_span = span == 0
            cur = pl.ds(pl.multiple_of(blk * BLK, BLK), BLK)
            back = pl.ds(pl.multiple_of(jnp.maximum(blk - d, 0) * BLK, BLK), BLK)
            edge = pl.ds(pl.multiple_of(res * BLK, BLK), BLK)
            q = q_ref[cur, :]
            k_cur = kc_ref[cur, :]
            v_cur = vc_ref[cur, :]
            k_prev = jnp.where(first_span, kp_ref[edge, :], kc_ref[back, :])
            v_prev = jnp.where(first_span, vp_ref[edge, :], vc_ref[back, :])
            prev_ok = jnp.logical_and(keep_prev,
                                      jnp.logical_not(jnp.logical_and(first_span, seq_start)))
            outs, lses = [], []
            for h in range(HEADS_PER_STEP):
                hc = slice(h * HEAD_DIM, (h + 1) * HEAD_DIM)
                nt = (((1,), (1,)), ((), ()))
                s_prev = lax.dot_general(q[:, hc], k_prev[:, hc], nt,
                                         preferred_element_type=F32)
                s_cur = lax.dot_general(q[:, hc], k_cur[:, hc], nt,
                                        preferred_element_type=F32)
                s_prev = jnp.where(prev_ok, s_prev + bias_ref[g, h, :, 0:BLK], NEG)
                s_cur = jnp.where(keep_cur, s_cur + bias_ref[g, h, :, BLK:2 * BLK], NEG)
                m = jnp.maximum(jnp.max(s_prev, axis=-1, keepdims=True),
                                jnp.max(s_cur, axis=-1, keepdims=True))
                p_prev = jnp.exp(s_prev - m)
                p_cur = jnp.exp(s_cur - m)
                den = (jnp.sum(p_prev, axis=-1, keepdims=True)
                       + jnp.sum(p_cur, axis=-1, keepdims=True))
                o = (jnp.dot(p_prev.astype(BF16), v_prev[:, hc], preferred_element_type=F32)
                     + jnp.dot(p_cur.astype(BF16), v_cur[:, hc], preferred_element_type=F32))
                outs.append(o / den)
                lses.append(jnp.broadcast_to(m + jnp.log(den), (BLK, HEAD_DIM)))
            start = span * (BLK * d) + res
            rows = pl.ds(start, BLK, stride=d) if d > 1 else cur
            for l in range(HEAD_COLS // V7X_LANES):
                pair = slice(l * HEADS_PER_LANE_TILE, (l + 1) * HEADS_PER_LANE_TILE)
                og_ref[g, l, rows, :] = jnp.concatenate(outs[pair], axis=-1)
                lse_ref[g, l, rows, :] = jnp.concatenate(lses[pair], axis=-1)
            return carry

        lax.fori_loop(0, n_blocks, block, 0)

    for l in range(HEAD_COLS // V7X_LANES):
        lse = [lse_ref[g, l] for g in range(N_GROUPS)]
        top = functools.reduce(jnp.maximum, lse)
        e = [jnp.exp(v - top) for v in lse]
        tot = e[0] + e[1] + e[2]
        mix = e[0] * og_ref[0, l] + e[1] * og_ref[1, l] + e[2] * og_ref[2, l]
        o_ref[:, l * V7X_LANES:(l + 1) * V7X_LANES] = (mix / tot).astype(o_ref.dtype)


def _dilated_attention(q_groups, kv_groups, bias, seq_len):
    n = q_groups[0].shape[0]
    tm = SPAN_TILE
    hcols = HEAD_COLS
    in_specs = [pl.BlockSpec((N_GROUPS, HEADS_PER_STEP, BLK, 2 * BLK),
                             lambda t, hh: (0, hh, 0, 0))]
    in_specs += [pl.BlockSpec((tm, hcols), lambda t, hh: (t, hh)) for _ in range(N_GROUPS)]
    args = [bias] + list(q_groups)
    for (_, d), kv in zip(DILATED_GROUPS, kv_groups):
        span = BLK * d
        per_tile = tm // span
        prev_map_k = lambda t, hh, p=per_tile: (jnp.maximum(t * p - 1, 0), hh)
        prev_map_v = lambda t, hh, p=per_tile: (jnp.maximum(t * p - 1, 0), HEAD_SPLIT + hh)
        in_specs += [
            pl.BlockSpec((tm, hcols), lambda t, hh: (t, hh)),
            pl.BlockSpec((tm, hcols), lambda t, hh: (t, HEAD_SPLIT + hh)),
            pl.BlockSpec((span, hcols), prev_map_k),
            pl.BlockSpec((span, hcols), prev_map_v),
        ]
        args += [kv, kv, kv, kv]
    return pl.pallas_call(
        functools.partial(_attn_kernel, tiles_per_seq=seq_len // tm),
        grid=(n // tm, HEAD_SPLIT),
        in_specs=in_specs,
        out_specs=pl.BlockSpec((tm, hcols), lambda t, hh: (t, hh)),
        out_shape=jax.ShapeDtypeStruct((n, KV_WIDTH), BF16),
        scratch_shapes=[pltpu.VMEM((N_GROUPS, hcols // V7X_LANES, tm, V7X_LANES), F32),
                        pltpu.VMEM((N_GROUPS, hcols // V7X_LANES, tm, V7X_LANES), F32)],
        compiler_params=_params("parallel", "parallel"),
        name="dilated_attention",
    )(*args)


def _out_proj_kernel(x_ref, o_ref_in, w_ref, g_ref, b_ref, out_ref):
    h = jnp.dot(o_ref_in[...], w_ref[...], preferred_element_type=F32)
    out_ref[...] = _layer_norm(ALPHA * x_ref[...] + h, g_ref[...], b_ref[...])


def _out_proj_layer(x2, o, w_o, g, b):
    n = x2.shape[0]
    tm = ROW_TILE
    return pl.pallas_call(
        _out_proj_kernel,
        grid=(n // tm,),
        in_specs=[pl.BlockSpec((tm, D_MODEL), lambda i: (i, 0)),
                  pl.BlockSpec((tm, KV_WIDTH), lambda i: (i, 0)),
                  _resident((KV_WIDTH, D_MODEL)),
                  _resident((1, D_MODEL)),
                  _resident((1, D_MODEL))],
        out_specs=pl.BlockSpec((tm, D_MODEL), lambda i: (i, 0)),
        out_shape=jax.ShapeDtypeStruct((n, D_MODEL), F32),
        compiler_params=_params("parallel"),
        name="attn_out_proj",
    )(x2, o, w_o.astype(BF16), g.reshape(1, -1), b.reshape(1, -1))


def kernel(x, a_w_in, a_ln_g, a_ln_b, a_w_s, a_b_s, a_w_out, kv_w, b_w_q, b_w_o,
           rel_table, ffn_w_up, ffn_conv_w, ffn_conv_b, ffn_w_down, ln_g, ln_b):
    B, T, _ = x.shape
    assert T % SPAN_TILE == 0 and T % ROW_TILE == 0
    x2 = x.reshape(B * T, D_MODEL)
    blocks_per_half = KV_WIDTH // V7X_MXU_DIM
    dils = [d for _, d in DILATED_GROUPS]
    kv_plan = tuple(tuple((o, c, d) for o, d in enumerate(dils))
                    for c in range(2 * blocks_per_half))
    q_plan = tuple(((g, c, dils[g]),) for g in range(N_GROUPS) for c in range(blocks_per_half))
    bias = kv_groups = None
    for i in range(DEPTH):
        if i < N_A_LAYERS:
            x2 = _mixer_a_layer(x2, a_w_in[i], a_ln_g[i], a_ln_b[i], a_w_s[i], a_b_s[i],
                                a_w_out[i], ln_g[i, 0], ln_b[i, 0])
        else:
            if i == N_A_LAYERS:
                kv_groups = _regroup_proj(x2, kv_w, kv_plan, [2 * KV_WIDTH] * N_GROUPS)
                bias = _band_bias(rel_table)
            j = i - N_A_LAYERS
            q_groups = _regroup_proj(x2, b_w_q[j], q_plan, [KV_WIDTH] * N_GROUPS,
                                     scale=HEAD_DIM ** -0.5)
            o = _dilated_attention(q_groups, kv_groups, bias, T)
            x2 = _out_proj_layer(x2, o, b_w_o[j], ln_g[i, 0], ln_b[i, 0])
        x2 = _conv_ffn_layer(x2, T, ffn_w_up[i], ffn_conv_w[i], ffn_conv_b[i],
                             ffn_w_down[i], ln_g[i, 1], ln_b[i, 1])
    return x2.reshape(B, T, D_MODEL)
```

```python
import functools
import math

import jax
import jax.numpy as jnp
from jax import lax
from jax.experimental import pallas as pl
from jax.experimental.pallas import tpu as pltpu

D_MODEL = 1024
DEPTH = 4
N_A_LAYERS = DEPTH // 2
CHUNK = 128
SGU_WIDTH = 2 * D_MODEL
SGU_GROUPS = 8
SGU_GROUP_WIDTH = SGU_WIDTH // SGU_GROUPS
HEAD_DIM = 64
KV_HEADS = D_MODEL // 128
KV_WIDTH = KV_HEADS * HEAD_DIM
DILATED_GROUPS = ((128, 1), (512, 4), (2048, 16))
N_GROUPS = len(DILATED_GROUPS)
BLK = 128
REL_BUCKETS = 32
REL_MAX_DIST = 2048
D_FF = 2816
CONV_WIDTH = 3
ALPHA = (2 * DEPTH) ** 0.25
LN_EPS = 1e-5
NEG = -1e30

F32 = jnp.float32
BF16 = jnp.bfloat16

V7X_SUBLANES = 8
V7X_LANES = 128
V7X_MXU_DIM = 256
V7X_VMEM_LIMIT_BYTES = 56 * 1024 * 1024

ROW_TILE = 512
FF_BLOCK = V7X_MXU_DIM
N_FF_BLOCKS = D_FF // FF_BLOCK
SPAN_TILE = max(w for w, _ in DILATED_GROUPS)
HEAD_SPLIT = 2
HEAD_COLS = KV_WIDTH // HEAD_SPLIT
HEADS_PER_STEP = KV_HEADS // HEAD_SPLIT
HEADS_PER_LANE_TILE = V7X_LANES // HEAD_DIM
MAX_BLOCK_UNROLL = 5
LOG2E = math.log2(math.e)

assert D_FF % FF_BLOCK == 0
assert all(w // d == BLK for w, d in DILATED_GROUPS)


def _gelu(x):
    return 0.5 * x * (1.0 + lax.erf(x * (1.0 / math.sqrt(2.0))))


def _layer_norm(x, g, b):
    mu = jnp.mean(x, axis=-1, keepdims=True)
    xc = x - mu
    var = jnp.mean(xc * xc, axis=-1, keepdims=True)
    return xc * lax.rsqrt(var + LN_EPS) * g + b


def _params(*semantics):
    return pltpu.CompilerParams(dimension_semantics=semantics,
                                vmem_limit_bytes=V7X_VMEM_LIMIT_BYTES)


def _resident(shape):
    zeros = (0,) * len(shape)
    return pl.BlockSpec(shape, lambda *_: zeros, pipeline_mode=pl.Buffered(1))


def _sgu_kernel(x_ref, win_ref, lng_ref, lnb_ref, ws_ref, bs_ref, wout_ref,
                g_ref, b_ref, o_ref, vn_ref, acc_ref):
    tm = x_ref.shape[0]
    x = x_ref[...]
    xb = x.astype(BF16)
    v = _gelu(jnp.dot(xb, win_ref[:, SGU_WIDTH:], preferred_element_type=F32))
    vn_ref[...] = _layer_norm(v, lng_ref[...], lnb_ref[...]).astype(BF16)

    row = lax.broadcasted_iota(jnp.int32, (CHUNK, CHUNK), 0)
    col = lax.broadcasted_iota(jnp.int32, (CHUNK, CHUNK), 1)
    causal = row >= col
    bs = bs_ref[...]
    acc_ref[...] = jnp.zeros_like(acc_ref)
    for g in range(SGU_GROUPS):
        cols = slice(g * SGU_GROUP_WIDTH, (g + 1) * SGU_GROUP_WIDTH)
        ws = jnp.where(causal, ws_ref[g], 0.0).astype(BF16)
        u = _gelu(jnp.dot(xb, win_ref[:, cols], preferred_element_type=F32))
        sv = [jnp.dot(ws, vn_ref[c * CHUNK:(c + 1) * CHUNK, cols],
                      preferred_element_type=F32) + bs[:, g:g + 1]
              for c in range(tm // CHUNK)]
        y = u * jnp.concatenate(sv, axis=0)
        acc_ref[...] += jnp.dot(y.astype(BF16), wout_ref[cols, :],
                                preferred_element_type=F32)
    o_ref[...] = _layer_norm(ALPHA * x + acc_ref[...], g_ref[...], b_ref[...])


def _mixer_a_layer(x2, w_in, ln_g, ln_b, w_s, b_s, w_out, g, b):
    n = x2.shape[0]
    tm = ROW_TILE
    return pl.pallas_call(
        _sgu_kernel,
        grid=(n // tm,),
        in_specs=[
            pl.BlockSpec((tm, D_MODEL), lambda i: (i, 0)),
            _resident((D_MODEL, 2 * SGU_WIDTH)),
            _resident((1, SGU_WIDTH)),
            _resident((1, SGU_WIDTH)),
            _resident((SGU_GROUPS, CHUNK, CHUNK)),
            _resident((CHUNK, SGU_GROUPS)),
            _resident((SGU_WIDTH, D_MODEL)),
            _resident((1, D_MODEL)),
            _resident((1, D_MODEL)),
        ],
        out_specs=pl.BlockSpec((tm, D_MODEL), lambda i: (i, 0)),
        out_shape=jax.ShapeDtypeStruct((n, D_MODEL), F32),
        scratch_shapes=[pltpu.VMEM((tm, SGU_WIDTH), BF16),
                        pltpu.VMEM((tm, D_MODEL), F32)],
        compiler_params=_params("parallel"),
        name="sgu_mixer",
    )(x2, w_in.astype(BF16), ln_g.reshape(1, -1), ln_b.reshape(1, -1), w_s,
      b_s.T, w_out.astype(BF16), g.reshape(1, -1), b.reshape(1, -1))


def _ffn_kernel(x_ref, wup_ref, cw_ref, cb_ref, wdn_ref, g_ref, b_ref, o_ref,
                hist_ref, carry_ref, acc_ref, *, tiles_per_seq):
    tm = x_ref.shape[0]
    pad = V7X_SUBLANES
    x = x_ref[...]
    xb = x.astype(BF16)
    seq_start = (pl.program_id(0) % tiles_per_seq) == 0
    acc_ref[...] = jnp.zeros_like(acc_ref)

    def conv_half(blk, slot):
        h = jnp.dot(xb, wup_ref[blk], preferred_element_type=F32)
        hist_ref[slot, 0:pad, :] = jnp.where(seq_start, 0.0, carry_ref[blk])
        hist_ref[slot, pad:pad + tm, :] = h
        carry_ref[blk] = h[tm - pad:tm]
        cw = cw_ref[blk]
        return (hist_ref[slot, pad - 2:pad - 2 + tm, :] * cw[0:1]
                + hist_ref[slot, pad - 1:pad - 1 + tm, :] * cw[1:2]
                + h * cw[2:3] + cb_ref[blk])

    def body(j, carry):
        a = conv_half(j, 0)
        gate = conv_half(N_FF_BLOCKS + j, 1)
        act = (_gelu(a) * gate).astype(BF16)
        acc_ref[...] += jnp.dot(act, wdn_ref[j], preferred_element_type=F32)
        return carry

    lax.fori_loop(0, N_FF_BLOCKS, body, 0, unroll=True)
    o_ref[...] = _layer_norm(ALPHA * x + acc_ref[...], g_ref[...], b_ref[...])


def _conv_ffn_layer(x2, seq_len, w_up, conv_w, conv_b, w_down, g, b):
    n = x2.shape[0]
    tm = ROW_TILE
    nb2 = 2 * N_FF_BLOCKS
    w_up_b = w_up.astype(BF16).reshape(D_MODEL, nb2, FF_BLOCK).transpose(1, 0, 2)
    cw_b = conv_w.reshape(CONV_WIDTH, nb2, FF_BLOCK).transpose(1, 0, 2)
    cb_b = conv_b.reshape(nb2, 1, FF_BLOCK)
    w_dn_b = w_down.astype(BF16).reshape(N_FF_BLOCKS, FF_BLOCK, D_MODEL)
    return pl.pallas_call(
        functools.partial(_ffn_kernel, tiles_per_seq=seq_len // tm),
        grid=(n // tm,),
        in_specs=[
            pl.BlockSpec((tm, D_MODEL), lambda i: (i, 0)),
            _resident((nb2, D_MODEL, FF_BLOCK)),
            _resident((nb2, CONV_WIDTH, FF_BLOCK)),
            _resident((nb2, 1, FF_BLOCK)),
            _resident((N_FF_BLOCKS, FF_BLOCK, D_MODEL)),
            _resident((1, D_MODEL)),
            _resident((1, D_MODEL)),
        ],
        out_specs=pl.BlockSpec((tm, D_MODEL), lambda i: (i, 0)),
        out_shape=jax.ShapeDtypeStruct((n, D_MODEL), F32),
        scratch_shapes=[
            pltpu.VMEM((2, tm + V7X_SUBLANES, FF_BLOCK), F32),
            pltpu.VMEM((nb2, V7X_SUBLANES, FF_BLOCK), F32),
            pltpu.VMEM((tm, D_MODEL), F32),
        ],
        compiler_params=_params("arbitrary"),
        name="conv_ffn",
    )(x2, w_up_b, cw_b, cb_b, w_dn_b, g.reshape(1, -1), b.reshape(1, -1))


def _regroup_proj_kernel(x_ref, w_ref, *rest, plan, scale):
    n_out = len({o for routes in plan for (o, _, _) in routes})
    out_refs, res_ref = rest[:n_out], rest[n_out]
    tm = x_ref.shape[0]
    cb = V7X_MXU_DIM
    xb = x_ref[...].astype(BF16)
    for c, routes in enumerate(plan):
        res = jnp.dot(xb, w_ref[:, c * cb:(c + 1) * cb], preferred_element_type=F32)
        if scale != 1.0:
            res = res * scale
        for l in range(cb // V7X_LANES):
            res_ref[l] = res[:, l * V7X_LANES:(l + 1) * V7X_LANES]
        for (o, oc, d) in routes:
            out = out_refs[o]
            if d == 1:
                out[:, oc * cb:(oc + 1) * cb] = res.astype(out.dtype)
                continue
            span = BLK * d
            for s in range(tm // span):
                for r in range(d):
                    rows = jnp.concatenate(
                        [res_ref[l, pl.ds(s * span + r, BLK, stride=d), :]
                         for l in range(cb // V7X_LANES)], axis=-1)
                    out[s * span + r * BLK:s * span + (r + 1) * BLK,
                        oc * cb:(oc + 1) * cb] = rows.astype(out.dtype)


def _regroup_proj(x2, w, plan, out_widths, scale=1.0):
    n = x2.shape[0]
    tm = SPAN_TILE
    return pl.pallas_call(
        functools.partial(_regroup_proj_kernel, plan=plan, scale=scale),
        grid=(n // tm,),
        in_specs=[pl.BlockSpec((tm, D_MODEL), lambda i: (i, 0)),
                  _resident(w.shape)],
        out_specs=[pl.BlockSpec((tm, wd), lambda i: (i, 0)) for wd in out_widths],
        out_shape=[jax.ShapeDtypeStruct((n, wd), BF16) for wd in out_widths],
        scratch_shapes=[pltpu.VMEM((V7X_MXU_DIM // V7X_LANES, tm, V7X_LANES), F32)],
        compiler_params=_params("parallel"),
        name="regroup_proj",
    )(x2, w.astype(BF16))


def _bias_kernel(table_ref, idx_ref, o_ref):
    g = pl.program_id(0)
    idx = idx_ref[0]
    row = lax.broadcasted_iota(jnp.int32, idx.shape, 0)
    col = lax.broadcasted_iota(jnp.int32, idx.shape, 1)
    delta = row + BLK - col
    in_band = jnp.logical_and(delta >= 0, delta <= BLK)
    in_band_cur = jnp.logical_and(in_band, col >= BLK)
    for h in range(KV_HEADS):
        acc = jnp.zeros(idx.shape, F32)
        for bkt in range(REL_BUCKETS):
            acc = jnp.where(idx == bkt,
                            table_ref[bkt * (N_GROUPS * KV_HEADS) + g * KV_HEADS + h], acc)
        acc = acc * LOG2E
        o_ref[0, 0, h] = jnp.where(in_band, acc, NEG)
        o_ref[1, 0, h] = jnp.where(in_band_cur, acc, NEG)


def _t5_bucket(dist):
    n = dist.astype(jnp.int32)
    max_exact = REL_BUCKETS // 2
    nf = jnp.maximum(n, 1).astype(F32)
    large = max_exact + (jnp.log(nf / max_exact) / math.log(REL_MAX_DIST / max_exact)
                         * (REL_BUCKETS - max_exact)).astype(jnp.int32)
    large = jnp.minimum(large, REL_BUCKETS - 1)
    return jnp.where(n < max_exact, n, large)


def _band_bias(rel_table):
    iq = jnp.arange(BLK)[:, None]
    ik = jnp.arange(2 * BLK)[None, :]
    delta = jnp.clip(iq + BLK - ik, 0, None)
    idx = jnp.stack([_t5_bucket(delta * d) for _, d in DILATED_GROUPS])
    return pl.pallas_call(
        _bias_kernel,
        grid=(N_GROUPS,),
        in_specs=[pl.BlockSpec(memory_space=pltpu.SMEM),
                  pl.BlockSpec((1, BLK, 2 * BLK), lambda g: (g, 0, 0))],
        out_specs=pl.BlockSpec((2, 1, KV_HEADS, BLK, 2 * BLK), lambda g: (0, g, 0, 0, 0)),
        out_shape=jax.ShapeDtypeStruct((2, N_GROUPS, KV_HEADS, BLK, 2 * BLK), F32),
        compiler_params=_params("parallel"),
        name="band_bias",
    )(rel_table.reshape(-1), idx)


def _attn_kernel(bias_ref, *refs, tiles_per_seq):
    q_refs = refs[0:N_GROUPS]
    kv_refs = refs[N_GROUPS:5 * N_GROUPS]
    o_ref = refs[5 * N_GROUPS]
    og_ref, lse_ref = refs[5 * N_GROUPS + 1:]
    tm = o_ref.shape[0]
    n_blocks = tm // BLK
    seq_start = (pl.program_id(0) % tiles_per_seq) == 0

    q_lane = lax.broadcasted_iota(jnp.int32, (BLK, V7X_LANES), 1) // HEAD_DIM
    kv_lane = lax.broadcasted_iota(jnp.int32, (2 * BLK, V7X_LANES), 1) // HEAD_DIM
    ones_cols = [jnp.where(kv_lane == i, 1.0, 0.0).astype(BF16)
                 for i in range(HEADS_PER_LANE_TILE)]
    nt = (((1,), (1,)), ((), ()))

    def block(blk, carry, *, g, d, first_span):
        q_ref = q_refs[g]
        kc_ref, vc_ref, kp_ref, vp_ref = kv_refs[4 * g:4 * g + 4]
        cur = pl.ds(pl.multiple_of(blk * BLK, BLK), BLK)
        if first_span:
            prev = pl.ds(pl.multiple_of(blk * BLK, BLK), BLK)
            kprev_ref, vprev_ref = kp_ref, vp_ref
            variant = seq_start.astype(jnp.int32)
            start = blk
        else:
            prev = pl.ds(pl.multiple_of((blk - d) * BLK, BLK), BLK)
            kprev_ref, vprev_ref = kc_ref, vc_ref
            variant = 0
            start = (blk // d) * (BLK * d) + blk % d
        rows = pl.ds(start, BLK, stride=d) if d > 1 else cur
        for l in range(HEAD_COLS // V7X_LANES):
            lanes = slice(l * V7X_LANES, (l + 1) * V7X_LANES)
            q = q_ref[cur, lanes]
            k_both = jnp.concatenate([kprev_ref[prev, lanes], kc_ref[cur, lanes]], axis=0)
            v_both = jnp.concatenate([vprev_ref[prev, lanes], vc_ref[cur, lanes]], axis=0)
            ps, vxs = [], []
            m_tile = None
            for i in range(HEADS_PER_LANE_TILE):
                h = l * HEADS_PER_LANE_TILE + i
                qm = jnp.where(q_lane == i, q, jnp.zeros_like(q))
                s = lax.dot_general(qm, k_both, nt, preferred_element_type=F32)
                s = s + bias_ref[variant, g, h]
                m = jnp.max(s, axis=-1, keepdims=True)
                ps.append(jnp.exp2(s - m).astype(BF16))
                vxs.append(jnp.concatenate(
                    [jnp.where(kv_lane == i, v_both, jnp.zeros_like(v_both)),
                     ones_cols[i]], axis=1))
                mb = jnp.broadcast_to(m, (BLK, V7X_LANES))
                m_tile = mb if m_tile is None else jnp.where(q_lane == i, mb, m_tile)
            acc = jnp.dot(jnp.concatenate(ps, axis=1), jnp.concatenate(vxs, axis=0),
                          preferred_element_type=F32)
            den = acc[:, V7X_LANES:]
            og_ref[g, l, rows, :] = acc[:, :V7X_LANES] / den
            lse_ref[g, l, rows, :] = m_tile + jnp.log2(den)
        return carry

    for g, (_, d) in enumerate(DILATED_GROUPS):
        for lo, hi, first_span in ((0, d, True), (d, n_blocks, False)):
            if hi > lo:
                unroll = max(u for u in range(1, MAX_BLOCK_UNROLL + 1) if (hi - lo) % u == 0)
                lax.fori_loop(lo, hi, functools.partial(block, g=g, d=d, first_span=first_span),
                              0, unroll=unroll)

    for l in range(HEAD_COLS // V7X_LANES):
        lse = [lse_ref[g, l] for g in range(N_GROUPS)]
        top = functools.reduce(jnp.maximum, lse)
        e = [jnp.exp2(v - top) for v in lse]
        tot = e[0] + e[1] + e[2]
        mix = e[0] * og_ref[0, l] + e[1] * og_ref[1, l] + e[2] * og_ref[2, l]
        o_ref[:, l * V7X_LANES:(l + 1) * V7X_LANES] = (mix / tot).astype(o_ref.dtype)


def _dilated_attention(q_groups, kv_groups, bias, seq_len):
    n = q_groups[0].shape[0]
    tm = SPAN_TILE
    hcols = HEAD_COLS
    in_specs = [pl.BlockSpec((2, N_GROUPS, HEADS_PER_STEP, BLK, 2 * BLK),
                             lambda t, hh: (0, 0, hh, 0, 0))]
    in_specs += [pl.BlockSpec((tm, hcols), lambda t, hh: (t, hh)) for _ in range(N_GROUPS)]
    args = [bias] + list(q_groups)
    for (_, d), kv in zip(DILATED_GROUPS, kv_groups):
        span = BLK * d
        per_tile = tm // span
        prev_map_k = lambda t, hh, p=per_tile: (jnp.maximum(t * p - 1, 0), hh)
        prev_map_v = lambda t, hh, p=per_tile: (jnp.maximum(t * p - 1, 0), HEAD_SPLIT + hh)
        in_specs += [
            pl.BlockSpec((tm, hcols), lambda t, hh: (t, hh)),
            pl.BlockSpec((tm, hcols), lambda t, hh: (t, HEAD_SPLIT + hh)),
            pl.BlockSpec((span, hcols), prev_map_k),
            pl.BlockSpec((span, hcols), prev_map_v),
        ]
        args += [kv, kv, kv, kv]
    return pl.pallas_call(
        functools.partial(_attn_kernel, tiles_per_seq=seq_len // tm),
        grid=(n // tm, HEAD_SPLIT),
        in_specs=in_specs,
        out_specs=pl.BlockSpec((tm, hcols), lambda t, hh: (t, hh)),
        out_shape=jax.ShapeDtypeStruct((n, KV_WIDTH), BF16),
        scratch_shapes=[pltpu.VMEM((N_GROUPS, hcols // V7X_LANES, tm, V7X_LANES), F32),
                        pltpu.VMEM((N_GROUPS, hcols // V7X_LANES, tm, V7X_LANES), F32)],
        compiler_params=_params("parallel", "parallel"),
        name="dilated_attention",
    )(*args)


def _out_proj_kernel(x_ref, o_ref_in, w_ref, g_ref, b_ref, out_ref):
    h = jnp.dot(o_ref_in[...], w_ref[...], preferred_element_type=F32)
    out_ref[...] = _layer_norm(ALPHA * x_ref[...] + h, g_ref[...], b_ref[...])


def _out_proj_layer(x2, o, w_o, g, b):
    n = x2.shape[0]
    tm = ROW_TILE
    return pl.pallas_call(
        _out_proj_kernel,
        grid=(n // tm,),
        in_specs=[pl.BlockSpec((tm, D_MODEL), lambda i: (i, 0)),
                  pl.BlockSpec((tm, KV_WIDTH), lambda i: (i, 0)),
                  _resident((KV_WIDTH, D_MODEL)),
                  _resident((1, D_MODEL)),
                  _resident((1, D_MODEL))],
        out_specs=pl.BlockSpec((tm, D_MODEL), lambda i: (i, 0)),
        out_shape=jax.ShapeDtypeStruct((n, D_MODEL), F32),
        compiler_params=_params("parallel"),
        name="attn_out_proj",
    )(x2, o, w_o.astype(BF16), g.reshape(1, -1), b.reshape(1, -1))


def kernel(x, a_w_in, a_ln_g, a_ln_b, a_w_s, a_b_s, a_w_out, kv_w, b_w_q, b_w_o,
           rel_table, ffn_w_up, ffn_conv_w, ffn_conv_b, ffn_w_down, ln_g, ln_b):
    B, T, _ = x.shape
    assert T % SPAN_TILE == 0 and T % ROW_TILE == 0
    x2 = x.reshape(B * T, D_MODEL)
    blocks_per_half = KV_WIDTH // V7X_MXU_DIM
    dils = [d for _, d in DILATED_GROUPS]
    kv_plan = tuple(tuple((o, c, d) for o, d in enumerate(dils))
                    for c in range(2 * blocks_per_half))
    q_plan = tuple(((g, c, dils[g]),) for g in range(N_GROUPS) for c in range(blocks_per_half))
    bias = kv_groups = None
    for i in range(DEPTH):
        if i < N_A_LAYERS:
            x2 = _mixer_a_layer(x2, a_w_in[i], a_ln_g[i], a_ln_b[i], a_w_s[i], a_b_s[i],
                                a_w_out[i], ln_g[i, 0], ln_b[i, 0])
        else:
            if i == N_A_LAYERS:
                kv_groups = _regroup_proj(x2, kv_w, kv_plan, [2 * KV_WIDTH] * N_GROUPS)
                bias = _band_bias(rel_table)
            j = i - N_A_LAYERS
            q_groups = _regroup_proj(x2, b_w_q[j], q_plan, [KV_WIDTH] * N_GROUPS,
                                     scale=HEAD_DIM ** -0.5 * LOG2E)
            o = _dilated_attention(q_groups, kv_groups, bias, T)
            x2 = _out_proj_layer(x2, o, b_w_o[j], ln_g[i, 0], ln_b[i, 0])
        x2 = _conv_ffn_layer(x2, T, ffn_w_up[i], ffn_conv_w[i], ffn_conv_b[i],
                             ffn_w_down[i], ln_g[i, 1], ln_b[i, 1])
    return x2.reshape(B, T, D_MODEL)
```

```python
import functools
import math

import jax
import jax.numpy as jnp
from jax import lax
from jax.experimental import pallas as pl
from jax.experimental.pallas import tpu as pltpu

D_MODEL = 1024
DEPTH = 4
N_A_LAYERS = DEPTH // 2
CHUNK = 128
SGU_WIDTH = 2 * D_MODEL
SGU_GROUPS = 8
SGU_GROUP_WIDTH = SGU_WIDTH // SGU_GROUPS
HEAD_DIM = 64
KV_HEADS = D_MODEL // 128
KV_WIDTH = KV_HEADS * HEAD_DIM
DILATED_GROUPS = ((128, 1), (512, 4), (2048, 16))
N_GROUPS = len(DILATED_GROUPS)
BLK = 128
REL_BUCKETS = 32
REL_MAX_DIST = 2048
D_FF = 2816
CONV_WIDTH = 3
ALPHA = (2 * DEPTH) ** 0.25
LN_EPS = 1e-5
NEG = -1e30

F32 = jnp.float32
BF16 = jnp.bfloat16

V7X_SUBLANES = 8
V7X_LANES = 128
V7X_MXU_DIM = 256
V7X_VMEM_LIMIT_BYTES = 56 * 1024 * 1024

ROW_TILE = 512
FF_BLOCK = V7X_MXU_DIM
N_FF_BLOCKS = D_FF // FF_BLOCK
SPAN_TILE = max(w for w, _ in DILATED_GROUPS)
HEAD_SPLIT = 2
HEAD_COLS = KV_WIDTH // HEAD_SPLIT
HEADS_PER_STEP = KV_HEADS // HEAD_SPLIT
HEADS_PER_LANE_TILE = V7X_LANES // HEAD_DIM
MAX_BLOCK_UNROLL = 5
LOG2E = math.log2(math.e)

assert D_FF % FF_BLOCK == 0
assert all(w // d == BLK for w, d in DILATED_GROUPS)


def _gelu(x):
    return 0.5 * x * (1.0 + lax.erf(x * (1.0 / math.sqrt(2.0))))


def _layer_norm(x, g, b):
    mu = jnp.mean(x, axis=-1, keepdims=True)
    xc = x - mu
    var = jnp.mean(xc * xc, axis=-1, keepdims=True)
    return xc * lax.rsqrt(var + LN_EPS) * g + b


def _params(*semantics):
    return pltpu.CompilerParams(dimension_semantics=semantics,
                                vmem_limit_bytes=V7X_VMEM_LIMIT_BYTES)


def _resident(shape):
    zeros = (0,) * len(shape)
    return pl.BlockSpec(shape, lambda *_: zeros, pipeline_mode=pl.Buffered(1))


def _sgu_kernel(x_ref, win_ref, lng_ref, lnb_ref, ws_ref, bs_ref, wout_ref,
                g_ref, b_ref, o_ref, vn_ref, acc_ref):
    tm = x_ref.shape[0]
    x = x_ref[...]
    xb = x.astype(BF16)
    v = _gelu(jnp.dot(xb, win_ref[:, SGU_WIDTH:], preferred_element_type=F32))
    vn_ref[...] = _layer_norm(v, lng_ref[...], lnb_ref[...]).astype(BF16)

    row = lax.broadcasted_iota(jnp.int32, (CHUNK, CHUNK), 0)
    col = lax.broadcasted_iota(jnp.int32, (CHUNK, CHUNK), 1)
    causal = row >= col
    bs = bs_ref[...]
    acc_ref[...] = jnp.zeros_like(acc_ref)
    for g in range(SGU_GROUPS):
        cols = slice(g * SGU_GROUP_WIDTH, (g + 1) * SGU_GROUP_WIDTH)
        ws = jnp.where(causal, ws_ref[g], 0.0).astype(BF16)
        u = _gelu(jnp.dot(xb, win_ref[:, cols], preferred_element_type=F32))
        sv = [jnp.dot(ws, vn_ref[c * CHUNK:(c + 1) * CHUNK, cols],
                      preferred_element_type=F32) + bs[:, g:g + 1]
              for c in range(tm // CHUNK)]
        y = u * jnp.concatenate(sv, axis=0)
        acc_ref[...] += jnp.dot(y.astype(BF16), wout_ref[cols, :],
                                preferred_element_type=F32)
    o_ref[...] = _layer_norm(ALPHA * x + acc_ref[...], g_ref[...], b_ref[...])


def _mixer_a_layer(x2, w_in, ln_g, ln_b, w_s, b_s, w_out, g, b):
    n = x2.shape[0]
    tm = ROW_TILE
    return pl.pallas_call(
        _sgu_kernel,
        grid=(n // tm,),
        in_specs=[
            pl.BlockSpec((tm, D_MODEL), lambda i: (i, 0)),
            _resident((D_MODEL, 2 * SGU_WIDTH)),
            _resident((1, SGU_WIDTH)),
            _resident((1, SGU_WIDTH)),
            _resident((SGU_GROUPS, CHUNK, CHUNK)),
            _resident((CHUNK, SGU_GROUPS)),
            _resident((SGU_WIDTH, D_MODEL)),
            _resident((1, D_MODEL)),
            _resident((1, D_MODEL)),
        ],
        out_specs=pl.BlockSpec((tm, D_MODEL), lambda i: (i, 0)),
        out_shape=jax.ShapeDtypeStruct((n, D_MODEL), F32),
        scratch_shapes=[pltpu.VMEM((tm, SGU_WIDTH), BF16),
                        pltpu.VMEM((tm, D_MODEL), F32)],
        compiler_params=_params("parallel"),
        name="sgu_mixer",
    )(x2, w_in.astype(BF16), ln_g.reshape(1, -1), ln_b.reshape(1, -1), w_s,
      b_s.T, w_out.astype(BF16), g.reshape(1, -1), b.reshape(1, -1))


def _ffn_kernel(x_ref, wup_ref, cw_ref, cb_ref, wdn_ref, g_ref, b_ref, o_ref,
                xs_ref, ys_ref, carry_ref, acc_ref, *, tiles_per_seq):
    tm = x_ref.shape[0]
    sub = V7X_SUBLANES
    vrows = tm // sub
    pitch = vrows + sub
    n_slabs = D_MODEL // V7X_LANES
    x = x_ref[...]
    seq_start = (pl.program_id(0) % tiles_per_seq) == 0

    for l in range(n_slabs):
        for s in range(sub):
            xs_ref[l, s * pitch:s * pitch + vrows, :] = (
                x[s * vrows:(s + 1) * vrows, l * V7X_LANES:(l + 1) * V7X_LANES])
    xb = jnp.concatenate(
        [jnp.concatenate([xs_ref[l, pl.ds(j, sub, stride=pitch), :] for j in range(vrows)],
                         axis=0) for l in range(n_slabs)], axis=1).astype(BF16)

    sublane = lax.broadcasted_iota(jnp.int32, (sub, FF_BLOCK), 0)
    acc_ref[...] = jnp.zeros_like(acc_ref)

    def conv_half(blk):
        h = jnp.dot(xb, wup_ref[:, blk * FF_BLOCK:(blk + 1) * FF_BLOCK],
                    preferred_element_type=F32)
        old = jnp.where(seq_start, 0.0, carry_ref[blk])
        carry_ref[blk] = h[tm - 2 * sub:tm]
        wrap1 = jnp.where(sublane == 0, pltpu.roll(old[sub:], 1, 0),
                          pltpu.roll(h[tm - sub:tm], 1, 0))
        wrap2 = jnp.where(sublane == 0, pltpu.roll(old[:sub], 1, 0),
                          pltpu.roll(h[tm - 2 * sub:tm - sub], 1, 0))
        back1 = jnp.concatenate([wrap1, h[:tm - sub]], axis=0)
        back2 = jnp.concatenate([wrap2, wrap1, h[:tm - 2 * sub]], axis=0)
        cw = cw_ref[blk]
        return back2 * cw[0:1] + back1 * cw[1:2] + h * cw[2:3] + cb_ref[blk]

    nxt = (conv_half(0), conv_half(N_FF_BLOCKS))
    for j in range(N_FF_BLOCKS):
        a, gate = nxt
        if j + 1 < N_FF_BLOCKS:
            nxt = (conv_half(j + 1), conv_half(N_FF_BLOCKS + j + 1))
        act = (a * (1.0 + lax.erf(a)) * gate).astype(BF16)
        acc_ref[...] += jnp.dot(act, wdn_ref[j * FF_BLOCK:(j + 1) * FF_BLOCK, :],
                                preferred_element_type=F32)

    for l in range(n_slabs):
        ys_ref[l] = acc_ref[:, l * V7X_LANES:(l + 1) * V7X_LANES]
    f = jnp.concatenate(
        [jnp.concatenate([ys_ref[l, pl.ds(s, vrows, stride=sub), :] for s in range(sub)],
                         axis=0) for l in range(n_slabs)], axis=1)
    o_ref[...] = _layer_norm(ALPHA * x + f, g_ref[...], b_ref[...])


def _conv_ffn_layer(x2, seq_len, w_up, conv_w, conv_b, w_down, g, b):
    n = x2.shape[0]
    tm = ROW_TILE
    nb2 = 2 * N_FF_BLOCKS
    n_slabs = D_MODEL // V7X_LANES
    pitch = tm // V7X_SUBLANES + V7X_SUBLANES
    half_scale = jnp.concatenate([jnp.full((D_FF,), math.sqrt(0.5), F32), jnp.ones((D_FF,), F32)])
    cw_b = (conv_w * half_scale).reshape(CONV_WIDTH, nb2, FF_BLOCK).transpose(1, 0, 2)
    cb_b = (conv_b * half_scale).reshape(nb2, 1, FF_BLOCK)
    w_dn_b = (w_down * math.sqrt(0.5)).astype(BF16)
    return pl.pallas_call(
        functools.partial(_ffn_kernel, tiles_per_seq=seq_len // tm),
        grid=(n // tm,),
        in_specs=[
            pl.BlockSpec((tm, D_MODEL), lambda i: (i, 0)),
            _resident((D_MODEL, 2 * D_FF)),
            _resident((nb2, CONV_WIDTH, FF_BLOCK)),
            _resident((nb2, 1, FF_BLOCK)),
            _resident((D_FF, D_MODEL)),
            _resident((1, D_MODEL)),
            _resident((1, D_MODEL)),
        ],
        out_specs=pl.BlockSpec((tm, D_MODEL), lambda i: (i, 0)),
        out_shape=jax.ShapeDtypeStruct((n, D_MODEL), F32),
        scratch_shapes=[
            pltpu.VMEM((n_slabs, V7X_SUBLANES * pitch, V7X_LANES), F32),
            pltpu.VMEM((n_slabs, tm, V7X_LANES), F32),
            pltpu.VMEM((nb2, 2 * V7X_SUBLANES, FF_BLOCK), F32),
            pltpu.VMEM((tm, D_MODEL), F32),
        ],
        compiler_params=_params("arbitrary"),
        name="conv_ffn",
    )(x2, w_up.astype(BF16), cw_b, cb_b, w_dn_b, g.reshape(1, -1), b.reshape(1, -1))


def _regroup_proj_kernel(x_ref, w_ref, *rest, plan, scale):
    n_out = len({o for routes in plan for (o, _, _) in routes})
    out_refs, res_ref = rest[:n_out], rest[n_out]
    tm = x_ref.shape[0]
    cb = V7X_MXU_DIM
    xb = x_ref[...].astype(BF16)
    for c, routes in enumerate(plan):
        res = jnp.dot(xb, w_ref[:, c * cb:(c + 1) * cb], preferred_element_type=F32)
        if scale != 1.0:
            res = res * scale
        for l in range(cb // V7X_LANES):
            res_ref[l] = res[:, l * V7X_LANES:(l + 1) * V7X_LANES]
        for (o, oc, d) in routes:
            out = out_refs[o]
            if d == 1:
                out[:, oc * cb:(oc + 1) * cb] = res.astype(out.dtype)
                continue
            span = BLK * d
            for s in range(tm // span):
                for r in range(d):
                    rows = jnp.concatenate(
                        [res_ref[l, pl.ds(s * span + r, BLK, stride=d), :]
                         for l in range(cb // V7X_LANES)], axis=-1)
                    out[s * span + r * BLK:s * span + (r + 1) * BLK,
                        oc * cb:(oc + 1) * cb] = rows.astype(out.dtype)


def _regroup_proj(x2, w, plan, out_widths, scale=1.0):
    n = x2.shape[0]
    tm = SPAN_TILE
    return pl.pallas_call(
        functools.partial(_regroup_proj_kernel, plan=plan, scale=scale),
        grid=(n // tm,),
        in_specs=[pl.BlockSpec((tm, D_MODEL), lambda i: (i, 0)),
                  _resident(w.shape)],
        out_specs=[pl.BlockSpec((tm, wd), lambda i: (i, 0)) for wd in out_widths],
        out_shape=[jax.ShapeDtypeStruct((n, wd), BF16) for wd in out_widths],
        scratch_shapes=[pltpu.VMEM((V7X_MXU_DIM // V7X_LANES, tm, V7X_LANES), F32)],
        compiler_params=_params("parallel"),
        name="regroup_proj",
    )(x2, w.astype(BF16))


def _bias_kernel(table_ref, idx_ref, o_ref):
    g = pl.program_id(0)
    idx = idx_ref[0]
    row = lax.broadcasted_iota(jnp.int32, idx.shape, 0)
    col = lax.broadcasted_iota(jnp.int32, idx.shape, 1)
    delta = row + BLK - col
    in_band = jnp.logical_and(delta >= 0, delta <= BLK)
    in_band_cur = jnp.logical_and(in_band, col >= BLK)
    for h in range(KV_HEADS):
        acc = jnp.zeros(idx.shape, F32)
        for bkt in range(REL_BUCKETS):
            acc = jnp.where(idx == bkt,
                            table_ref[bkt * (N_GROUPS * KV_HEADS) + g * KV_HEADS + h], acc)
        acc = acc * LOG2E
        o_ref[0, 0, h] = jnp.where(in_band, acc, NEG)
        o_ref[1, 0, h] = jnp.where(in_band_cur, acc, NEG)


def _t5_bucket(dist):
    n = dist.astype(jnp.int32)
    max_exact = REL_BUCKETS // 2
    nf = jnp.maximum(n, 1).astype(F32)
    large = max_exact + (jnp.log(nf / max_exact) / math.log(REL_MAX_DIST / max_exact)
                         * (REL_BUCKETS - max_exact)).astype(jnp.int32)
    large = jnp.minimum(large, REL_BUCKETS - 1)
    return jnp.where(n < max_exact, n, large)


def _band_bias(rel_table):
    iq = jnp.arange(BLK)[:, None]
    ik = jnp.arange(2 * BLK)[None, :]
    delta = jnp.clip(iq + BLK - ik, 0, None)
    idx = jnp.stack([_t5_bucket(delta * d) for _, d in DILATED_GROUPS])
    return pl.pallas_call(
        _bias_kernel,
        grid=(N_GROUPS,),
        in_specs=[pl.BlockSpec(memory_space=pltpu.SMEM),
                  pl.BlockSpec((1, BLK, 2 * BLK), lambda g: (g, 0, 0))],
        out_specs=pl.BlockSpec((2, 1, KV_HEADS, BLK, 2 * BLK), lambda g: (0, g, 0, 0, 0)),
        out_shape=jax.ShapeDtypeStruct((2, N_GROUPS, KV_HEADS, BLK, 2 * BLK), F32),
        compiler_params=_params("parallel"),
        name="band_bias",
    )(rel_table.reshape(-1), idx)


def _attn_kernel(bias_ref, *refs, tiles_per_seq):
    q_refs = refs[0:N_GROUPS]
    kv_refs = refs[N_GROUPS:5 * N_GROUPS]
    o_ref = refs[5 * N_GROUPS]
    og_ref, lse_ref = refs[5 * N_GROUPS + 1:]
    tm = o_ref.shape[0]
    n_blocks = tm // BLK
    seq_start = (pl.program_id(0) % tiles_per_seq) == 0

    q_lane = lax.broadcasted_iota(jnp.int32, (BLK, V7X_LANES), 1) // HEAD_DIM
    kv_lane = lax.broadcasted_iota(jnp.int32, (2 * BLK, V7X_LANES), 1) // HEAD_DIM
    ones_cols = [jnp.where(kv_lane == i, 1.0, 0.0).astype(BF16)
                 for i in range(HEADS_PER_LANE_TILE)]
    nt = (((1,), (1,)), ((), ()))

    def block(blk, carry, *, g, d, first_span):
        q_ref = q_refs[g]
        kc_ref, vc_ref, kp_ref, vp_ref = kv_refs[4 * g:4 * g + 4]
        cur = pl.ds(pl.multiple_of(blk * BLK, BLK), BLK)
        if first_span:
            prev = pl.ds(pl.multiple_of(blk * BLK, BLK), BLK)
            kprev_ref, vprev_ref = kp_ref, vp_ref
            variant = seq_start.astype(jnp.int32)
            start = blk
        else:
            prev = pl.ds(pl.multiple_of((blk - d) * BLK, BLK), BLK)
            kprev_ref, vprev_ref = kc_ref, vc_ref
            variant = 0
            start = (blk // d) * (BLK * d) + blk % d
        rows = pl.ds(start, BLK, stride=d) if d > 1 else cur
        for l in range(HEAD_COLS // V7X_LANES):
            lanes = slice(l * V7X_LANES, (l + 1) * V7X_LANES)
            q = q_ref[cur, lanes]
            k_both = jnp.concatenate([kprev_ref[prev, lanes], kc_ref[cur, lanes]], axis=0)
            v_both = jnp.concatenate([vprev_ref[prev, lanes], vc_ref[cur, lanes]], axis=0)
            ps, vxs = [], []
            m_tile = None
            for i in range(HEADS_PER_LANE_TILE):
                h = l * HEADS_PER_LANE_TILE + i
                qm = jnp.where(q_lane == i, q, jnp.zeros_like(q))
                s = lax.dot_general(qm, k_both, nt, preferred_element_type=F32)
                s = s + bias_ref[variant, g, h]
                m = jnp.max(s, axis=-1, keepdims=True)
                ps.append(jnp.exp2(s - m).astype(BF16))
                vxs.append(jnp.concatenate(
                    [jnp.where(kv_lane == i, v_both, jnp.zeros_like(v_both)),
                     ones_cols[i]], axis=1))
                mb = jnp.broadcast_to(m, (BLK, V7X_LANES))
                m_tile = mb if m_tile is None else jnp.where(q_lane == i, mb, m_tile)
            acc = jnp.dot(jnp.concatenate(ps, axis=1), jnp.concatenate(vxs, axis=0),
                          preferred_element_type=F32)
            den = acc[:, V7X_LANES:]
            og_ref[g, l, rows, :] = acc[:, :V7X_LANES] / den
            lse_ref[g, l, rows, :] = m_tile + jnp.log2(den)
        return carry

    for g, (_, d) in enumerate(DILATED_GROUPS):
        for lo, hi, first_span in ((0, d, True), (d, n_blocks, False)):
            if hi > lo:
                unroll = max(u for u in range(1, MAX_BLOCK_UNROLL + 1) if (hi - lo) % u == 0)
                lax.fori_loop(lo, hi, functools.partial(block, g=g, d=d, first_span=first_span),
                              0, unroll=unroll)

    for l in range(HEAD_COLS // V7X_LANES):
        lse = [lse_ref[g, l] for g in range(N_GROUPS)]
        top = functools.reduce(jnp.maximum, lse)
        e = [jnp.exp2(v - top) for v in lse]
        tot = e[0] + e[1] + e[2]
        mix = e[0] * og_ref[0, l] + e[1] * og_ref[1, l] + e[2] * og_ref[2, l]
        o_ref[:, l * V7X_LANES:(l + 1) * V7X_LANES] = (mix / tot).astype(o_ref.dtype)


def _dilated_attention(q_groups, kv_groups, bias, seq_len):
    n = q_groups[0].shape[0]
    tm = SPAN_TILE
    hcols = HEAD_COLS
    in_specs = [pl.BlockSpec((2, N_GROUPS, HEADS_PER_STEP, BLK, 2 * BLK),
                             lambda t, hh: (0, 0, hh, 0, 0))]
    in_specs += [pl.BlockSpec((tm, hcols), lambda t, hh: (t, hh)) for _ in range(N_GROUPS)]
    args = [bias] + list(q_groups)
    for (_, d), kv in zip(DILATED_GROUPS, kv_groups):
        span = BLK * d
        per_tile = tm // span
        prev_map_k = lambda t, hh, p=per_tile: (jnp.maximum(t * p - 1, 0), hh)
        prev_map_v = lambda t, hh, p=per_tile: (jnp.maximum(t * p - 1, 0), HEAD_SPLIT + hh)
        in_specs += [
            pl.BlockSpec((tm, hcols), lambda t, hh: (t, hh)),
            pl.BlockSpec((tm, hcols), lambda t, hh: (t, HEAD_SPLIT + hh)),
            pl.BlockSpec((span, hcols), prev_map_k),
            pl.BlockSpec((span, hcols), prev_map_v),
        ]
        args += [kv, kv, kv, kv]
    return pl.pallas_call(
        functools.partial(_attn_kernel, tiles_per_seq=seq_len // tm),
        grid=(n // tm, HEAD_SPLIT),
        in_specs=in_specs,
        out_specs=pl.BlockSpec((tm, hcols), lambda t, hh: (t, hh)),
        out_shape=jax.ShapeDtypeStruct((n, KV_WIDTH), BF16),
        scratch_shapes=[pltpu.VMEM((N_GROUPS, hcols // V7X_LANES, tm, V7X_LANES), F32),
                        pltpu.VMEM((N_GROUPS, hcols // V7X_LANES, tm, V7X_LANES), F32)],
        compiler_params=_params("parallel", "parallel"),
        name="dilated_attention",
    )(*args)


def _out_proj_kernel(x_ref, o_ref_in, w_ref, g_ref, b_ref, out_ref):
    h = jnp.dot(o_ref_in[...], w_ref[...], preferred_element_type=F32)
    out_ref[...] = _layer_norm(ALPHA * x_ref[...] + h, g_ref[...], b_ref[...])


def _out_proj_layer(x2, o, w_o, g, b):
    n = x2.shape[0]
    tm = ROW_TILE
    return pl.pallas_call(
        _out_proj_kernel,
        grid=(n // tm,),
        in_specs=[pl.BlockSpec((tm, D_MODEL), lambda i: (i, 0)),
                  pl.BlockSpec((tm, KV_WIDTH), lambda i: (i, 0)),
                  _resident((KV_WIDTH, D_MODEL)),
                  _resident((1, D_MODEL)),
                  _resident((1, D_MODEL))],
        out_specs=pl.BlockSpec((tm, D_MODEL), lambda i: (i, 0)),
        out_shape=jax.ShapeDtypeStruct((n, D_MODEL), F32),
        compiler_params=_params("parallel"),
        name="attn_out_proj",
    )(x2, o, w_o.astype(BF16), g.reshape(1, -1), b.reshape(1, -1))


def kernel(x, a_w_in, a_ln_g, a_ln_b, a_w_s, a_b_s, a_w_out, kv_w, b_w_q, b_w_o,
           rel_table, ffn_w_up, ffn_conv_w, ffn_conv_b, ffn_w_down, ln_g, ln_b):
    B, T, _ = x.shape
    assert T % SPAN_TILE == 0 and T % ROW_TILE == 0
    x2 = x.reshape(B * T, D_MODEL)
    blocks_per_half = KV_WIDTH // V7X_MXU_DIM
    dils = [d for _, d in DILATED_GROUPS]
    kv_plan = tuple(tuple((o, c, d) for o, d in enumerate(dils))
                    for c in range(2 * blocks_per_half))
    q_plan = tuple(((g, c, dils[g]),) for g in range(N_GROUPS) for c in range(blocks_per_half))
    bias = kv_groups = None
    for i in range(DEPTH):
        if i < N_A_LAYERS:
            x2 = _mixer_a_layer(x2, a_w_in[i], a_ln_g[i], a_ln_b[i], a_w_s[i], a_b_s[i],
                                a_w_out[i], ln_g[i, 0], ln_b[i, 0])
        else:
            if i == N_A_LAYERS:
                kv_groups = _regroup_proj(x2, kv_w, kv_plan, [2 * KV_WIDTH] * N_GROUPS)
                bias = _band_bias(rel_table)
            j = i - N_A_LAYERS
            q_groups = _regroup_proj(x2, b_w_q[j], q_plan, [KV_WIDTH] * N_GROUPS,
                                     scale=HEAD_DIM ** -0.5 * LOG2E)
            o = _dilated_attention(q_groups, kv_groups, bias, T)
            x2 = _out_proj_layer(x2, o, b_w_o[j], ln_g[i, 0], ln_b[i, 0])
        x2 = _conv_ffn_layer(x2, T, ffn_w_up[i], ffn_conv_w[i], ffn_conv_b[i],
                             ffn_w_down[i], ln_g[i, 1], ln_b[i, 1])
    return x2.reshape(B, T, D_MODEL)
```

```python
import functools
import math

import jax
import jax.numpy as jnp
from jax import lax
from jax.experimental import pallas as pl
from jax.experimental.pallas import tpu as pltpu

D_MODEL = 1024
DEPTH = 4
N_A_LAYERS = DEPTH // 2
CHUNK = 128
SGU_WIDTH = 2 * D_MODEL
SGU_GROUPS = 8
SGU_GROUP_WIDTH = SGU_WIDTH // SGU_GROUPS
HEAD_DIM = 64
KV_HEADS = D_MODEL // 128
KV_WIDTH = KV_HEADS * HEAD_DIM
DILATED_GROUPS = ((128, 1), (512, 4), (2048, 16))
N_GROUPS = len(DILATED_GROUPS)
BLK = 128
REL_BUCKETS = 32
REL_MAX_DIST = 2048
D_FF = 2816
CONV_WIDTH = 3
ALPHA = (2 * DEPTH) ** 0.25
LN_EPS = 1e-5
NEG = -1e30

F32 = jnp.float32
BF16 = jnp.bfloat16

V7X_SUBLANES = 8
V7X_LANES = 128
V7X_MXU_DIM = 256
V7X_VMEM_LIMIT_BYTES = 56 * 1024 * 1024

ROW_TILE = 512
FF_BLOCK = D_FF
N_FF_BLOCKS = D_FF // FF_BLOCK
SPAN_TILE = max(w for w, _ in DILATED_GROUPS)
HEAD_SPLIT = 2
HEAD_COLS = KV_WIDTH // HEAD_SPLIT
HEADS_PER_STEP = KV_HEADS // HEAD_SPLIT
HEADS_PER_LANE_TILE = V7X_LANES // HEAD_DIM
MAX_BLOCK_UNROLL = 5
LOG2E = math.log2(math.e)

assert D_FF % FF_BLOCK == 0
assert all(w // d == BLK for w, d in DILATED_GROUPS)


def _gelu(x):
    return 0.5 * x * (1.0 + lax.erf(x * (1.0 / math.sqrt(2.0))))


def _layer_norm(x, g, b):
    mu = jnp.mean(x, axis=-1, keepdims=True)
    xc = x - mu
    var = jnp.mean(xc * xc, axis=-1, keepdims=True)
    return xc * lax.rsqrt(var + LN_EPS) * g + b


def _params(*semantics):
    return pltpu.CompilerParams(dimension_semantics=semantics,
                                vmem_limit_bytes=V7X_VMEM_LIMIT_BYTES)


def _resident(shape):
    zeros = (0,) * len(shape)
    return pl.BlockSpec(shape, lambda *_: zeros, pipeline_mode=pl.Buffered(1))


def _sgu_kernel(x_ref, win_ref, lng_ref, lnb_ref, ws_ref, bs_ref, wout_ref,
                g_ref, b_ref, o_ref, vn_ref):
    tm = x_ref.shape[0]
    x = x_ref[...]
    xb = x.astype(BF16)
    zv = jnp.dot(xb, win_ref[:, SGU_WIDTH:], preferred_element_type=F32)
    zu = jnp.dot(xb, win_ref[:, :SGU_WIDTH], preferred_element_type=F32)
    v = (math.sqrt(0.5) * zv) * (1.0 + lax.erf(zv))
    vn_ref[...] = _layer_norm(v, lng_ref[...], lnb_ref[...]).astype(BF16)
    u = zu * (1.0 + lax.erf(zu))

    row = lax.broadcasted_iota(jnp.int32, (CHUNK, CHUNK), 0)
    col = lax.broadcasted_iota(jnp.int32, (CHUNK, CHUNK), 1)
    causal = row >= col
    sv = []
    for g in range(SGU_GROUPS):
        cols = slice(g * SGU_GROUP_WIDTH, (g + 1) * SGU_GROUP_WIDTH)
        ws = jnp.where(causal, ws_ref[g], 0.0).astype(BF16)
        sv.append(jnp.concatenate(
            [jnp.dot(ws, vn_ref[c * CHUNK:(c + 1) * CHUNK, cols], preferred_element_type=F32)
             + bs_ref[:, cols] for c in range(tm // CHUNK)], axis=0))
    y = (u * jnp.concatenate(sv, axis=1)).astype(BF16)
    h = jnp.dot(y, wout_ref[...], preferred_element_type=F32)
    o_ref[...] = _layer_norm(ALPHA * x + h, g_ref[...], b_ref[...])


def _mixer_a_layer(x2, w_in, ln_g, ln_b, w_s, b_s, w_out, g, b):
    n = x2.shape[0]
    tm = ROW_TILE
    return pl.pallas_call(
        _sgu_kernel,
        grid=(n // tm,),
        in_specs=[
            pl.BlockSpec((tm, D_MODEL), lambda i: (i, 0)),
            _resident((D_MODEL, 2 * SGU_WIDTH)),
            _resident((1, SGU_WIDTH)),
            _resident((1, SGU_WIDTH)),
            _resident((SGU_GROUPS, CHUNK, CHUNK)),
            _resident((CHUNK, SGU_WIDTH)),
            _resident((SGU_WIDTH, D_MODEL)),
            _resident((1, D_MODEL)),
            _resident((1, D_MODEL)),
        ],
        out_specs=pl.BlockSpec((tm, D_MODEL), lambda i: (i, 0)),
        out_shape=jax.ShapeDtypeStruct((n, D_MODEL), F32),
        scratch_shapes=[pltpu.VMEM((tm, SGU_WIDTH), BF16)],
        compiler_params=_params("parallel"),
        name="sgu_mixer",
    )(x2, (w_in * math.sqrt(0.5)).astype(BF16), ln_g.reshape(1, -1), ln_b.reshape(1, -1), w_s,
      jnp.repeat(b_s.T, SGU_GROUP_WIDTH, axis=1), (w_out * math.sqrt(0.5)).astype(BF16),
      g.reshape(1, -1), b.reshape(1, -1))


def _ffn_kernel(x_ref, wup_ref, cw_ref, cb_ref, wdn_ref, g_ref, b_ref, o_ref,
                xs_ref, ys_ref, carry_ref, acc_ref, *, tiles_per_seq):
    tm = x_ref.shape[0]
    sub = V7X_SUBLANES
    vrows = tm // sub
    pitch = vrows + sub
    n_slabs = D_MODEL // V7X_LANES
    x = x_ref[...]
    seq_start = (pl.program_id(0) % tiles_per_seq) == 0

    for l in range(n_slabs):
        for s in range(sub):
            xs_ref[l, s * pitch:s * pitch + vrows, :] = (
                x[s * vrows:(s + 1) * vrows, l * V7X_LANES:(l + 1) * V7X_LANES])
    xb = jnp.concatenate(
        [jnp.concatenate([xs_ref[l, pl.ds(j, sub, stride=pitch), :] for j in range(vrows)],
                         axis=0) for l in range(n_slabs)], axis=1).astype(BF16)

    sublane = lax.broadcasted_iota(jnp.int32, (sub, FF_BLOCK), 0)
    acc_ref[...] = jnp.zeros_like(acc_ref)

    def conv_half(blk):
        h = jnp.dot(xb, wup_ref[:, blk * FF_BLOCK:(blk + 1) * FF_BLOCK],
                    preferred_element_type=F32)
        old = jnp.where(seq_start, 0.0, carry_ref[blk])
        carry_ref[blk] = h[tm - 2 * sub:tm]
        wrap1 = jnp.where(sublane == 0, pltpu.roll(old[sub:], 1, 0),
                          pltpu.roll(h[tm - sub:tm], 1, 0))
        wrap2 = jnp.where(sublane == 0, pltpu.roll(old[:sub], 1, 0),
                          pltpu.roll(h[tm - 2 * sub:tm - sub], 1, 0))
        back1 = jnp.concatenate([wrap1, h[:tm - sub]], axis=0)
        back2 = jnp.concatenate([wrap2, wrap1, h[:tm - 2 * sub]], axis=0)
        cw = cw_ref[blk]
        return back2 * cw[0:1] + back1 * cw[1:2] + h * cw[2:3] + cb_ref[blk]

    nxt = (conv_half(0), conv_half(N_FF_BLOCKS))
    for j in range(N_FF_BLOCKS):
        a, gate = nxt
        if j + 1 < N_FF_BLOCKS:
            nxt = (conv_half(j + 1), conv_half(N_FF_BLOCKS + j + 1))
        act = (a * (1.0 + lax.erf(a)) * gate).astype(BF16)
        acc_ref[...] += jnp.dot(act, wdn_ref[j * FF_BLOCK:(j + 1) * FF_BLOCK, :],
                                preferred_element_type=F32)

    for l in range(n_slabs):
        ys_ref[l] = acc_ref[:, l * V7X_LANES:(l + 1) * V7X_LANES]
    f = jnp.concatenate(
        [jnp.concatenate([ys_ref[l, pl.ds(s, vrows, stride=sub), :] for s in range(sub)],
                         axis=0) for l in range(n_slabs)], axis=1)
    o_ref[...] = _layer_norm(ALPHA * x + f, g_ref[...], b_ref[...])


def _conv_ffn_layer(x2, seq_len, w_up, conv_w, conv_b, w_down, g, b):
    n = x2.shape[0]
    tm = ROW_TILE
    nb2 = 2 * N_FF_BLOCKS
    n_slabs = D_MODEL // V7X_LANES
    pitch = tm // V7X_SUBLANES + V7X_SUBLANES
    half_scale = jnp.concatenate([jnp.full((D_FF,), math.sqrt(0.5), F32), jnp.ones((D_FF,), F32)])
    cw_b = (conv_w * half_scale).reshape(CONV_WIDTH, nb2, FF_BLOCK).transpose(1, 0, 2)
    cb_b = (conv_b * half_scale).reshape(nb2, 1, FF_BLOCK)
    w_dn_b = (w_down * math.sqrt(0.5)).astype(BF16)
    return pl.pallas_call(
        functools.partial(_ffn_kernel, tiles_per_seq=seq_len // tm),
        grid=(n // tm,),
        in_specs=[
            pl.BlockSpec((tm, D_MODEL), lambda i: (i, 0)),
            _resident((D_MODEL, 2 * D_FF)),
            _resident((nb2, CONV_WIDTH, FF_BLOCK)),
            _resident((nb2, 1, FF_BLOCK)),
            _resident((D_FF, D_MODEL)),
            _resident((1, D_MODEL)),
            _resident((1, D_MODEL)),
        ],
        out_specs=pl.BlockSpec((tm, D_MODEL), lambda i: (i, 0)),
        out_shape=jax.ShapeDtypeStruct((n, D_MODEL), F32),
        scratch_shapes=[
            pltpu.VMEM((n_slabs, V7X_SUBLANES * pitch, V7X_LANES), F32),
            pltpu.VMEM((n_slabs, tm, V7X_LANES), F32),
            pltpu.VMEM((nb2, 2 * V7X_SUBLANES, FF_BLOCK), F32),
            pltpu.VMEM((tm, D_MODEL), F32),
        ],
        compiler_params=_params("arbitrary"),
        name="conv_ffn",
    )(x2, w_up.astype(BF16), cw_b, cb_b, w_dn_b, g.reshape(1, -1), b.reshape(1, -1))


def _regroup_proj_kernel(x_ref, w_ref, *rest, plan, scale):
    n_out = len({o for routes in plan for (o, _, _) in routes})
    out_refs, res_ref = rest[:n_out], rest[n_out]
    tm = x_ref.shape[0]
    cb = V7X_MXU_DIM
    xb = x_ref[...].astype(BF16)
    for c, routes in enumerate(plan):
        res = jnp.dot(xb, w_ref[:, c * cb:(c + 1) * cb], preferred_element_type=F32)
        if scale != 1.0:
            res = res * scale
        for l in range(cb // V7X_LANES):
            res_ref[l] = res[:, l * V7X_LANES:(l + 1) * V7X_LANES]
        for (o, oc, d) in routes:
            out = out_refs[o]
            if d == 1:
                out[:, oc * cb:(oc + 1) * cb] = res.astype(out.dtype)
                continue
            span = BLK * d
            for s in range(tm // span):
                for r in range(d):
                    rows = jnp.concatenate(
                        [res_ref[l, pl.ds(s * span + r, BLK, stride=d), :]
                         for l in range(cb // V7X_LANES)], axis=-1)
                    out[s * span + r * BLK:s * span + (r + 1) * BLK,
                        oc * cb:(oc + 1) * cb] = rows.astype(out.dtype)


def _regroup_proj(x2, w, plan, out_widths, scale=1.0):
    n = x2.shape[0]
    tm = SPAN_TILE
    return pl.pallas_call(
        functools.partial(_regroup_proj_kernel, plan=plan, scale=scale),
        grid=(n // tm,),
        in_specs=[pl.BlockSpec((tm, D_MODEL), lambda i: (i, 0)),
                  _resident(w.shape)],
        out_specs=[pl.BlockSpec((tm, wd), lambda i: (i, 0)) for wd in out_widths],
        out_shape=[jax.ShapeDtypeStruct((n, wd), BF16) for wd in out_widths],
        scratch_shapes=[pltpu.VMEM((V7X_MXU_DIM // V7X_LANES, tm, V7X_LANES), F32)],
        compiler_params=_params("parallel"),
        name="regroup_proj",
    )(x2, w.astype(BF16))


def _bias_kernel(table_ref, idx_ref, o_ref):
    g = pl.program_id(0)
    idx = idx_ref[0]
    row = lax.broadcasted_iota(jnp.int32, idx.shape, 0)
    col = lax.broadcasted_iota(jnp.int32, idx.shape, 1)
    delta = row + BLK - col
    in_band = jnp.logical_and(delta >= 0, delta <= BLK)
    in_band_cur = jnp.logical_and(in_band, col >= BLK)
    for h in range(KV_HEADS):
        acc = jnp.zeros(idx.shape, F32)
        for bkt in range(REL_BUCKETS):
            acc = jnp.where(idx == bkt,
                            table_ref[bkt * (N_GROUPS * KV_HEADS) + g * KV_HEADS + h], acc)
        acc = acc * LOG2E
        o_ref[0, 0, h] = jnp.where(in_band, acc, NEG)
        o_ref[1, 0, h] = jnp.where(in_band_cur, acc, NEG)


def _t5_bucket(dist):
    n = dist.astype(jnp.int32)
    max_exact = REL_BUCKETS // 2
    nf = jnp.maximum(n, 1).astype(F32)
    large = max_exact + (jnp.log(nf / max_exact) / math.log(REL_MAX_DIST / max_exact)
                         * (REL_BUCKETS - max_exact)).astype(jnp.int32)
    large = jnp.minimum(large, REL_BUCKETS - 1)
    return jnp.where(n < max_exact, n, large)


def _band_bias(rel_table):
    iq = jnp.arange(BLK)[:, None]
    ik = jnp.arange(2 * BLK)[None, :]
    delta = jnp.clip(iq + BLK - ik, 0, None)
    idx = jnp.stack([_t5_bucket(delta * d) for _, d in DILATED_GROUPS])
    return pl.pallas_call(
        _bias_kernel,
        grid=(N_GROUPS,),
        in_specs=[pl.BlockSpec(memory_space=pltpu.SMEM),
                  pl.BlockSpec((1, BLK, 2 * BLK), lambda g: (g, 0, 0))],
        out_specs=pl.BlockSpec((2, 1, KV_HEADS, BLK, 2 * BLK), lambda g: (0, g, 0, 0, 0)),
        out_shape=jax.ShapeDtypeStruct((2, N_GROUPS, KV_HEADS, BLK, 2 * BLK), F32),
        compiler_params=_params("parallel"),
        name="band_bias",
    )(rel_table.reshape(-1), idx)


def _attn_kernel(bias_ref, *refs, tiles_per_seq):
    q_refs = refs[0:N_GROUPS]
    kv_refs = refs[N_GROUPS:5 * N_GROUPS]
    o_ref = refs[5 * N_GROUPS]
    og_ref, lse_ref = refs[5 * N_GROUPS + 1:]
    tm = o_ref.shape[0]
    n_blocks = tm // BLK
    seq_start = (pl.program_id(0) % tiles_per_seq) == 0

    q_lane = lax.broadcasted_iota(jnp.int32, (BLK, V7X_LANES), 1) // HEAD_DIM
    kv_lane = lax.broadcasted_iota(jnp.int32, (2 * BLK, V7X_LANES), 1) // HEAD_DIM
    ones_cols = [jnp.where(kv_lane == i, 1.0, 0.0).astype(BF16)
                 for i in range(HEADS_PER_LANE_TILE)]
    nt = (((1,), (1,)), ((), ()))

    def block(blk, carry, *, g, d, first_span):
        q_ref = q_refs[g]
        kc_ref, vc_ref, kp_ref, vp_ref = kv_refs[4 * g:4 * g + 4]
        cur = pl.ds(pl.multiple_of(blk * BLK, BLK), BLK)
        if first_span:
            prev = pl.ds(pl.multiple_of(blk * BLK, BLK), BLK)
            kprev_ref, vprev_ref = kp_ref, vp_ref
            variant = seq_start.astype(jnp.int32)
            start = blk
        else:
            prev = pl.ds(pl.multiple_of((blk - d) * BLK, BLK), BLK)
            kprev_ref, vprev_ref = kc_ref, vc_ref
            variant = 0
            start = (blk // d) * (BLK * d) + blk % d
        rows = pl.ds(start, BLK, stride=d) if d > 1 else cur
        for l in range(HEAD_COLS // V7X_LANES):
            lanes = slice(l * V7X_LANES, (l + 1) * V7X_LANES)
            q = q_ref[cur, lanes]
            k_both = jnp.concatenate([kprev_ref[prev, lanes], kc_ref[cur, lanes]], axis=0)
            v_both = jnp.concatenate([vprev_ref[prev, lanes], vc_ref[cur, lanes]], axis=0)
            ps, vxs = [], []
            m_tile = None
            for i in range(HEADS_PER_LANE_TILE):
                h = l * HEADS_PER_LANE_TILE + i
                qm = jnp.where(q_lane == i, q, jnp.zeros_like(q))
                s = lax.dot_general(qm, k_both, nt, preferred_element_type=F32)
                s = s + bias_ref[variant, g, h]
                m = jnp.max(s, axis=-1, keepdims=True)
                ps.append(jnp.exp2(s - m).astype(BF16))
                vxs.append(jnp.concatenate(
                    [jnp.where(kv_lane == i, v_both, jnp.zeros_like(v_both)),
                     ones_cols[i]], axis=1))
                mb = jnp.broadcast_to(m, (BLK, V7X_LANES))
                m_tile = mb if m_tile is None else jnp.where(q_lane == i, mb, m_tile)
            acc = jnp.dot(jnp.concatenate(ps, axis=1), jnp.concatenate(vxs, axis=0),
                          preferred_element_type=F32)
            den = acc[:, V7X_LANES:]
            og_ref[g, l, rows, :] = acc[:, :V7X_LANES] / den
            lse_ref[g, l, rows, :] = m_tile + jnp.log2(den)
        return carry

    for g, (_, d) in enumerate(DILATED_GROUPS):
        for lo, hi, first_span in ((0, d, True), (d, n_blocks, False)):
            if hi > lo:
                unroll = max(u for u in range(1, MAX_BLOCK_UNROLL + 1) if (hi - lo) % u == 0)
                lax.fori_loop(lo, hi, functools.partial(block, g=g, d=d, first_span=first_span),
                              0, unroll=unroll)

    for l in range(HEAD_COLS // V7X_LANES):
        lse = [lse_ref[g, l] for g in range(N_GROUPS)]
        top = functools.reduce(jnp.maximum, lse)
        e = [jnp.exp2(v - top) for v in lse]
        tot = e[0] + e[1] + e[2]
        mix = e[0] * og_ref[0, l] + e[1] * og_ref[1, l] + e[2] * og_ref[2, l]
        o_ref[:, l * V7X_LANES:(l + 1) * V7X_LANES] = (mix / tot).astype(o_ref.dtype)


def _dilated_attention(q_groups, kv_groups, bias, seq_len):
    n = q_groups[0].shape[0]
    tm = SPAN_TILE
    hcols = HEAD_COLS
    in_specs = [pl.BlockSpec((2, N_GROUPS, HEADS_PER_STEP, BLK, 2 * BLK),
                             lambda t, hh: (0, 0, hh, 0, 0))]
    in_specs += [pl.BlockSpec((tm, hcols), lambda t, hh: (t, hh)) for _ in range(N_GROUPS)]
    args = [bias] + list(q_groups)
    for (_, d), kv in zip(DILATED_GROUPS, kv_groups):
        span = BLK * d
        per_tile = tm // span
        prev_map_k = lambda t, hh, p=per_tile: (jnp.maximum(t * p - 1, 0), hh)
        prev_map_v = lambda t, hh, p=per_tile: (jnp.maximum(t * p - 1, 0), HEAD_SPLIT + hh)
        in_specs += [
            pl.BlockSpec((tm, hcols), lambda t, hh: (t, hh)),
            pl.BlockSpec((tm, hcols), lambda t, hh: (t, HEAD_SPLIT + hh)),
            pl.BlockSpec((span, hcols), prev_map_k),
            pl.BlockSpec((span, hcols), prev_map_v),
        ]
        args += [kv, kv, kv, kv]
    return pl.pallas_call(
        functools.partial(_attn_kernel, tiles_per_seq=seq_len // tm),
        grid=(n // tm, HEAD_SPLIT),
        in_specs=in_specs,
        out_specs=pl.BlockSpec((tm, hcols), lambda t, hh: (t, hh)),
        out_shape=jax.ShapeDtypeStruct((n, KV_WIDTH), BF16),
        scratch_shapes=[pltpu.VMEM((N_GROUPS, hcols // V7X_LANES, tm, V7X_LANES), F32),
                        pltpu.VMEM((N_GROUPS, hcols // V7X_LANES, tm, V7X_LANES), F32)],
        compiler_params=_params("parallel", "parallel"),
        name="dilated_attention",
    )(*args)


def _out_proj_kernel(x_ref, o_ref_in, w_ref, g_ref, b_ref, out_ref):
    h = jnp.dot(o_ref_in[...], w_ref[...], preferred_element_type=F32)
    out_ref[...] = _layer_norm(ALPHA * x_ref[...] + h, g_ref[...], b_ref[...])


def _out_proj_layer(x2, o, w_o, g, b):
    n = x2.shape[0]
    tm = ROW_TILE
    return pl.pallas_call(
        _out_proj_kernel,
        grid=(n // tm,),
        in_specs=[pl.BlockSpec((tm, D_MODEL), lambda i: (i, 0)),
                  pl.BlockSpec((tm, KV_WIDTH), lambda i: (i, 0)),
                  _resident((KV_WIDTH, D_MODEL)),
                  _resident((1, D_MODEL)),
                  _resident((1, D_MODEL))],
        out_specs=pl.BlockSpec((tm, D_MODEL), lambda i: (i, 0)),
        out_shape=jax.ShapeDtypeStruct((n, D_MODEL), F32),
        compiler_params=_params("parallel"),
        name="attn_out_proj",
    )(x2, o, w_o.astype(BF16), g.reshape(1, -1), b.reshape(1, -1))


def kernel(x, a_w_in, a_ln_g, a_ln_b, a_w_s, a_b_s, a_w_out, kv_w, b_w_q, b_w_o,
           rel_table, ffn_w_up, ffn_conv_w, ffn_conv_b, ffn_w_down, ln_g, ln_b):
    B, T, _ = x.shape
    assert T % SPAN_TILE == 0 and T % ROW_TILE == 0
    x2 = x.reshape(B * T, D_MODEL)
    blocks_per_half = KV_WIDTH // V7X_MXU_DIM
    dils = [d for _, d in DILATED_GROUPS]
    kv_plan = tuple(tuple((o, c, d) for o, d in enumerate(dils))
                    for c in range(2 * blocks_per_half))
    q_plan = tuple(((g, c, dils[g]),) for g in range(N_GROUPS) for c in range(blocks_per_half))
    bias = kv_groups = None
    for i in range(DEPTH):
        if i < N_A_LAYERS:
            x2 = _mixer_a_layer(x2, a_w_in[i], a_ln_g[i], a_ln_b[i], a_w_s[i], a_b_s[i],
                                a_w_out[i], ln_g[i, 0], ln_b[i, 0])
        else:
            if i == N_A_LAYERS:
                kv_groups = _regroup_proj(x2, kv_w, kv_plan, [2 * KV_WIDTH] * N_GROUPS)
                bias = _band_bias(rel_table)
            j = i - N_A_LAYERS
            q_groups = _regroup_proj(x2, b_w_q[j], q_plan, [KV_WIDTH] * N_GROUPS,
                                     scale=HEAD_DIM ** -0.5 * LOG2E)
            o = _dilated_attention(q_groups, kv_groups, bias, T)
            x2 = _out_proj_layer(x2, o, b_w_o[j], ln_g[i, 0], ln_b[i, 0])
        x2 = _conv_ffn_layer(x2, T, ffn_w_up[i], ffn_conv_w[i], ffn_conv_b[i],
                             ffn_w_down[i], ln_g[i, 1], ln_b[i, 1])
    return x2.reshape(B, T, D_MODEL)
```

```python
import functools
import math

import jax
import jax.numpy as jnp
from jax import lax
from jax.experimental import pallas as pl
from jax.experimental.pallas import tpu as pltpu

D_MODEL = 1024
DEPTH = 4
N_A_LAYERS = DEPTH // 2
CHUNK = 128
SGU_WIDTH = 2 * D_MODEL
SGU_GROUPS = 8
SGU_GROUP_WIDTH = SGU_WIDTH // SGU_GROUPS
HEAD_DIM = 64
KV_HEADS = D_MODEL // 128
KV_WIDTH = KV_HEADS * HEAD_DIM
DILATED_GROUPS = ((128, 1), (512, 4), (2048, 16))
N_GROUPS = len(DILATED_GROUPS)
BLK = 128
REL_BUCKETS = 32
REL_MAX_DIST = 2048
D_FF = 2816
CONV_WIDTH = 3
ALPHA = (2 * DEPTH) ** 0.25
LN_EPS = 1e-5
NEG = -1e30

F32 = jnp.float32
BF16 = jnp.bfloat16

V7X_SUBLANES = 8
V7X_LANES = 128
V7X_MXU_DIM = 256
V7X_VMEM_LIMIT_BYTES = 56 * 1024 * 1024

ROW_TILE = 512
FFN_SUB_TILE = 512
FFN_SUB_TILES = 1
SPAN_TILE = max(w for w, _ in DILATED_GROUPS)
HEAD_SPLIT = 2
HEAD_COLS = KV_WIDTH // HEAD_SPLIT
HEADS_PER_STEP = KV_HEADS // HEAD_SPLIT
HEADS_PER_LANE_TILE = V7X_LANES // HEAD_DIM
MAX_BLOCK_UNROLL = 16
LOG2E = math.log2(math.e)

assert all(w // d == BLK for w, d in DILATED_GROUPS)


def _gelu(x):
    return 0.5 * x * (1.0 + lax.erf(x * (1.0 / math.sqrt(2.0))))


def _layer_norm(x, g, b):
    mu = jnp.mean(x, axis=-1, keepdims=True)
    xc = x - mu
    var = jnp.mean(xc * xc, axis=-1, keepdims=True)
    return xc * lax.rsqrt(var + LN_EPS) * g + b


def _params(*semantics):
    return pltpu.CompilerParams(dimension_semantics=semantics,
                                vmem_limit_bytes=V7X_VMEM_LIMIT_BYTES)


def _resident(shape):
    zeros = (0,) * len(shape)
    return pl.BlockSpec(shape, lambda *_: zeros, pipeline_mode=pl.Buffered(1))


def _sgu_kernel(x_ref, win_ref, lng_ref, lnb_ref, ws_ref, bs_ref, wout_ref,
                g_ref, b_ref, o_ref, vn_ref):
    tm = x_ref.shape[0]
    x = x_ref[...]
    xb = x.astype(BF16)
    zv = jnp.dot(xb, win_ref[:, SGU_WIDTH:], preferred_element_type=F32)
    zu = jnp.dot(xb, win_ref[:, :SGU_WIDTH], preferred_element_type=F32)
    v = (math.sqrt(0.5) * zv) * (1.0 + lax.erf(zv))
    vn_ref[...] = _layer_norm(v, lng_ref[...], lnb_ref[...]).astype(BF16)
    u = zu * (1.0 + lax.erf(zu))

    row = lax.broadcasted_iota(jnp.int32, (CHUNK, CHUNK), 0)
    col = lax.broadcasted_iota(jnp.int32, (CHUNK, CHUNK), 1)
    causal = row >= col
    sv = []
    for g in range(SGU_GROUPS):
        cols = slice(g * SGU_GROUP_WIDTH, (g + 1) * SGU_GROUP_WIDTH)
        ws = jnp.where(causal, ws_ref[g], 0.0).astype(BF16)
        sv.append(jnp.concatenate(
            [jnp.dot(ws, vn_ref[c * CHUNK:(c + 1) * CHUNK, cols], preferred_element_type=F32)
             + bs_ref[:, cols] for c in range(tm // CHUNK)], axis=0))
    y = (u * jnp.concatenate(sv, axis=1)).astype(BF16)
    h = jnp.dot(y, wout_ref[...], preferred_element_type=F32)
    o_ref[...] = _layer_norm(ALPHA * x + h, g_ref[...], b_ref[...])


def _mixer_a_layer(x2, w_in, ln_g, ln_b, w_s, b_s, w_out, g, b):
    n = x2.shape[0]
    tm = ROW_TILE
    return pl.pallas_call(
        _sgu_kernel,
        grid=(n // tm,),
        in_specs=[
            pl.BlockSpec((tm, D_MODEL), lambda i: (i, 0)),
            _resident((D_MODEL, 2 * SGU_WIDTH)),
            _resident((1, SGU_WIDTH)),
            _resident((1, SGU_WIDTH)),
            _resident((SGU_GROUPS, CHUNK, CHUNK)),
            _resident((CHUNK, SGU_WIDTH)),
            _resident((SGU_WIDTH, D_MODEL)),
            _resident((1, D_MODEL)),
            _resident((1, D_MODEL)),
        ],
        out_specs=pl.BlockSpec((tm, D_MODEL), lambda i: (i, 0)),
        out_shape=jax.ShapeDtypeStruct((n, D_MODEL), F32),
        scratch_shapes=[pltpu.VMEM((tm, SGU_WIDTH), BF16)],
        compiler_params=_params("parallel"),
        name="sgu_mixer",
    )(x2, (w_in * math.sqrt(0.5)).astype(BF16), ln_g.reshape(1, -1), ln_b.reshape(1, -1), w_s,
      jnp.repeat(b_s.T, SGU_GROUP_WIDTH, axis=1), (w_out * math.sqrt(0.5)).astype(BF16),
      g.reshape(1, -1), b.reshape(1, -1))


def _ffn_kernel(x_ref, wup_ref, cw_ref, cb_ref, wdn_ref, g_ref, b_ref, o_ref,
                xs_ref, ys_ref, carry_ref, *, tiles_per_seq):
    tm = FFN_SUB_TILE
    sub = V7X_SUBLANES
    vrows = tm // sub
    pitch = vrows + sub
    n_slabs = D_MODEL // V7X_LANES
    seq_start = (pl.program_id(0) % tiles_per_seq) == 0
    sublane = lax.broadcasted_iota(jnp.int32, (sub, 2 * D_FF), 0)
    old = jnp.where(seq_start, 0.0, carry_ref[...])

    n_sub_tiles = x_ref.shape[0] // tm
    xbs = []
    for t in range(n_sub_tiles):
        x = x_ref[t * tm:(t + 1) * tm, :]
        for l in range(n_slabs):
            for s in range(sub):
                xs_ref[t, l, s * pitch:s * pitch + vrows, :] = (
                    x[s * vrows:(s + 1) * vrows, l * V7X_LANES:(l + 1) * V7X_LANES])
        xbs.append(jnp.concatenate(
            [jnp.concatenate([xs_ref[t, l, pl.ds(j, sub, stride=pitch), :]
                              for j in range(vrows)], axis=0)
             for l in range(n_slabs)], axis=1).astype(BF16))

    for t in range(n_sub_tiles):
        x = x_ref[t * tm:(t + 1) * tm, :]
        h = jnp.dot(xbs[t], wup_ref[...], preferred_element_type=F32)
        wrap1 = jnp.where(sublane == 0, pltpu.roll(old[sub:], 1, 0),
                          pltpu.roll(h[tm - sub:tm], 1, 0))
        wrap2 = jnp.where(sublane == 0, pltpu.roll(old[:sub], 1, 0),
                          pltpu.roll(h[tm - 2 * sub:tm - sub], 1, 0))
        old = h[tm - 2 * sub:tm]
        back1 = jnp.concatenate([wrap1, h[:tm - sub]], axis=0)
        back2 = jnp.concatenate([wrap2, wrap1, h[:tm - 2 * sub]], axis=0)
        hc = back2 * cw_ref[0:1] + back1 * cw_ref[1:2] + h * cw_ref[2:3] + cb_ref[...]
        a, gate = hc[:, :D_FF], hc[:, D_FF:]
        act = (a * (1.0 + lax.erf(a)) * gate).astype(BF16)
        f_perm = jnp.dot(act, wdn_ref[...], preferred_element_type=F32)

        for l in range(n_slabs):
            ys_ref[t, l] = f_perm[:, l * V7X_LANES:(l + 1) * V7X_LANES]
        f = jnp.concatenate(
            [jnp.concatenate([ys_ref[t, l, pl.ds(s, vrows, stride=sub), :] for s in range(sub)],
                             axis=0) for l in range(n_slabs)], axis=1)
        o_ref[t * tm:(t + 1) * tm, :] = _layer_norm(ALPHA * x + f, g_ref[...], b_ref[...])
    carry_ref[...] = old


def _conv_ffn_layer(x2, seq_len, w_up, conv_w, conv_b, w_down, g, b):
    n = x2.shape[0]
    tm = FFN_SUB_TILE * FFN_SUB_TILES
    n_slabs = D_MODEL // V7X_LANES
    pitch = FFN_SUB_TILE // V7X_SUBLANES + V7X_SUBLANES
    half_scale = jnp.concatenate([jnp.full((D_FF,), math.sqrt(0.5), F32), jnp.ones((D_FF,), F32)])
    return pl.pallas_call(
        functools.partial(_ffn_kernel, tiles_per_seq=seq_len // tm),
        grid=(n // tm,),
        in_specs=[
            pl.BlockSpec((tm, D_MODEL), lambda i: (i, 0)),
            _resident((D_MODEL, 2 * D_FF)),
            _resident((CONV_WIDTH, 2 * D_FF)),
            _resident((1, 2 * D_FF)),
            _resident((D_FF, D_MODEL)),
            _resident((1, D_MODEL)),
            _resident((1, D_MODEL)),
        ],
        out_specs=pl.BlockSpec((tm, D_MODEL), lambda i: (i, 0)),
        out_shape=jax.ShapeDtypeStruct((n, D_MODEL), F32),
        scratch_shapes=[
            pltpu.VMEM((FFN_SUB_TILES, n_slabs, V7X_SUBLANES * pitch, V7X_LANES), F32),
            pltpu.VMEM((FFN_SUB_TILES, n_slabs, FFN_SUB_TILE, V7X_LANES), F32),
            pltpu.VMEM((2 * V7X_SUBLANES, 2 * D_FF), F32),
        ],
        compiler_params=_params("arbitrary"),
        name="conv_ffn",
    )(x2, w_up.astype(BF16), conv_w * half_scale, (conv_b * half_scale).reshape(1, -1),
      (w_down * math.sqrt(0.5)).astype(BF16), g.reshape(1, -1), b.reshape(1, -1))


def _regroup_proj_kernel(x_ref, w_ref, *rest, plan, scale):
    n_out = len({o for routes in plan for (o, _, _) in routes})
    out_refs, res_ref = rest[:n_out], rest[n_out]
    tm = x_ref.shape[0]
    cb = V7X_MXU_DIM
    xb = x_ref[...].astype(BF16)
    for c, routes in enumerate(plan):
        res = jnp.dot(xb, w_ref[:, c * cb:(c + 1) * cb], preferred_element_type=F32)
        if scale != 1.0:
            res = res * scale
        for l in range(cb // V7X_LANES):
            res_ref[l] = res[:, l * V7X_LANES:(l + 1) * V7X_LANES]
        for (o, oc, d) in routes:
            out = out_refs[o]
            if d == 1:
                out[:, oc * cb:(oc + 1) * cb] = res.astype(out.dtype)
                continue
            span = BLK * d
            for s in range(tm // span):
                for r in range(d):
                    rows = jnp.concatenate(
                        [res_ref[l, pl.ds(s * span + r, BLK, stride=d), :]
                         for l in range(cb // V7X_LANES)], axis=-1)
                    out[s * span + r * BLK:s * span + (r + 1) * BLK,
                        oc * cb:(oc + 1) * cb] = rows.astype(out.dtype)


def _regroup_proj(x2, w, plan, out_widths, scale=1.0):
    n = x2.shape[0]
    tm = SPAN_TILE
    return pl.pallas_call(
        functools.partial(_regroup_proj_kernel, plan=plan, scale=scale),
        grid=(n // tm,),
        in_specs=[pl.BlockSpec((tm, D_MODEL), lambda i: (i, 0)),
                  _resident(w.shape)],
        out_specs=[pl.BlockSpec((tm, wd), lambda i: (i, 0)) for wd in out_widths],
        out_shape=[jax.ShapeDtypeStruct((n, wd), BF16) for wd in out_widths],
        scratch_shapes=[pltpu.VMEM((V7X_MXU_DIM // V7X_LANES, tm, V7X_LANES), F32)],
        compiler_params=_params("parallel"),
        name="regroup_proj",
    )(x2, w.astype(BF16))


def _bias_kernel(table_ref, idx_ref, o_ref):
    g = pl.program_id(0)
    idx = idx_ref[0]
    row = lax.broadcasted_iota(jnp.int32, idx.shape, 0)
    col = lax.broadcasted_iota(jnp.int32, idx.shape, 1)
    delta = row + BLK - col
    in_band = jnp.logical_and(delta >= 0, delta <= BLK)
    in_band_cur = jnp.logical_and(in_band, col >= BLK)
    for h in range(KV_HEADS):
        acc = jnp.zeros(idx.shape, F32)
        for bkt in range(REL_BUCKETS):
            acc = jnp.where(idx == bkt,
                            table_ref[bkt * (N_GROUPS * KV_HEADS) + g * KV_HEADS + h], acc)
        acc = acc * LOG2E
        o_ref[0, 0, h] = jnp.where(in_band, acc, NEG)
        o_ref[1, 0, h] = jnp.where(in_band_cur, acc, NEG)


def _t5_bucket(dist):
    n = dist.astype(jnp.int32)
    max_exact = REL_BUCKETS // 2
    nf = jnp.maximum(n, 1).astype(F32)
    large = max_exact + (jnp.log(nf / max_exact) / math.log(REL_MAX_DIST / max_exact)
                         * (REL_BUCKETS - max_exact)).astype(jnp.int32)
    large = jnp.minimum(large, REL_BUCKETS - 1)
    return jnp.where(n < max_exact, n, large)


def _band_bias(rel_table):
    iq = jnp.arange(BLK)[:, None]
    ik = jnp.arange(2 * BLK)[None, :]
    delta = jnp.clip(iq + BLK - ik, 0, None)
    idx = jnp.stack([_t5_bucket(delta * d) for _, d in DILATED_GROUPS])
    return pl.pallas_call(
        _bias_kernel,
        grid=(N_GROUPS,),
        in_specs=[pl.BlockSpec(memory_space=pltpu.SMEM),
                  pl.BlockSpec((1, BLK, 2 * BLK), lambda g: (g, 0, 0))],
        out_specs=pl.BlockSpec((2, 1, KV_HEADS, BLK, 2 * BLK), lambda g: (0, g, 0, 0, 0)),
        out_shape=jax.ShapeDtypeStruct((2, N_GROUPS, KV_HEADS, BLK, 2 * BLK), F32),
        compiler_params=_params("parallel"),
        name="band_bias",
    )(rel_table.reshape(-1), idx)


def _attn_kernel(bias_ref, *refs, tiles_per_seq):
    q_refs = refs[0:N_GROUPS]
    kv_refs = refs[N_GROUPS:5 * N_GROUPS]
    o_ref = refs[5 * N_GROUPS]
    og_ref, lse_ref = refs[5 * N_GROUPS + 1:]
    tm = o_ref.shape[0]
    n_blocks = tm // BLK
    seq_start = (pl.program_id(0) % tiles_per_seq) == 0

    q_lane = lax.broadcasted_iota(jnp.int32, (BLK, V7X_LANES), 1) // HEAD_DIM
    kv_lane = lax.broadcasted_iota(jnp.int32, (2 * BLK, V7X_LANES), 1) // HEAD_DIM
    ones_cols = [jnp.where(kv_lane == i, 1.0, 0.0).astype(BF16)
                 for i in range(HEADS_PER_LANE_TILE)]
    nt = (((1,), (1,)), ((), ()))

    def block(blk, carry, *, g, d, first_span):
        q_ref = q_refs[g]
        kc_ref, vc_ref, kp_ref, vp_ref = kv_refs[4 * g:4 * g + 4]
        cur = pl.ds(pl.multiple_of(blk * BLK, BLK), BLK)
        if first_span:
            prev = pl.ds(pl.multiple_of(blk * BLK, BLK), BLK)
            kprev_ref, vprev_ref = kp_ref, vp_ref
            variant = seq_start.astype(jnp.int32)
            start = blk
        else:
            prev = pl.ds(pl.multiple_of((blk - d) * BLK, BLK), BLK)
            kprev_ref, vprev_ref = kc_ref, vc_ref
            variant = 0
            start = (blk // d) * (BLK * d) + blk % d
        rows = pl.ds(start, BLK, stride=d) if d > 1 else cur
        for l in range(HEAD_COLS // V7X_LANES):
            lanes = slice(l * V7X_LANES, (l + 1) * V7X_LANES)
            q = q_ref[cur, lanes]
            k_both = jnp.concatenate([kprev_ref[prev, lanes], kc_ref[cur, lanes]], axis=0)
            v_both = jnp.concatenate([vprev_ref[prev, lanes], vc_ref[cur, lanes]], axis=0)
            ps, vxs = [], []
            m_tile = None
            for i in range(HEADS_PER_LANE_TILE):
                h = l * HEADS_PER_LANE_TILE + i
                qm = jnp.where(q_lane == i, q, jnp.zeros_like(q))
                s = lax.dot_general(qm, k_both, nt, preferred_element_type=F32)
                s = s + bias_ref[variant, g, h]
                m = jnp.max(s, axis=-1, keepdims=True)
                ps.append(jnp.exp2(s - m).astype(BF16))
                vxs.append(jnp.concatenate(
                    [jnp.where(kv_lane == i, v_both, jnp.zeros_like(v_both)),
                     ones_cols[i]], axis=1))
                mb = jnp.broadcast_to(m, (BLK, V7X_LANES))
                m_tile = mb if m_tile is None else jnp.where(q_lane == i, mb, m_tile)
            acc = jnp.dot(jnp.concatenate(ps, axis=1), jnp.concatenate(vxs, axis=0),
                          preferred_element_type=F32)
            den = acc[:, V7X_LANES:]
            og_ref[g, l, rows, :] = acc[:, :V7X_LANES] / den
            lse_ref[g, l, rows, :] = m_tile + jnp.log2(den)
        return carry

    for g, (_, d) in enumerate(DILATED_GROUPS):
        for lo, hi, first_span in ((0, d, True), (d, n_blocks, False)):
            if hi > lo:
                unroll = max(u for u in range(1, MAX_BLOCK_UNROLL + 1) if (hi - lo) % u == 0)
                lax.fori_loop(lo, hi, functools.partial(block, g=g, d=d, first_span=first_span),
                              0, unroll=unroll)

    for l in range(HEAD_COLS // V7X_LANES):
        lse = [lse_ref[g, l] for g in range(N_GROUPS)]
        top = functools.reduce(jnp.maximum, lse)
        e = [jnp.exp2(v - top) for v in lse]
        tot = e[0] + e[1] + e[2]
        mix = e[0] * og_ref[0, l] + e[1] * og_ref[1, l] + e[2] * og_ref[2, l]
        o_ref[:, l * V7X_LANES:(l + 1) * V7X_LANES] = (mix / tot).astype(o_ref.dtype)


def _dilated_attention(q_groups, kv_groups, bias, seq_len):
    n = q_groups[0].shape[0]
    tm = SPAN_TILE
    hcols = HEAD_COLS
    in_specs = [pl.BlockSpec((2, N_GROUPS, HEADS_PER_STEP, BLK, 2 * BLK),
                             lambda t, hh: (0, 0, hh, 0, 0))]
    in_specs += [pl.BlockSpec((tm, hcols), lambda t, hh: (t, hh)) for _ in range(N_GROUPS)]
    args = [bias] + list(q_groups)
    for (_, d), kv in zip(DILATED_GROUPS, kv_groups):
        span = BLK * d
        per_tile = tm // span
        prev_map_k = lambda t, hh, p=per_tile: (jnp.maximum(t * p - 1, 0), hh)
        prev_map_v = lambda t, hh, p=per_tile: (jnp.maximum(t * p - 1, 0), HEAD_SPLIT + hh)
        in_specs += [
            pl.BlockSpec((tm, hcols), lambda t, hh: (t, hh)),
            pl.BlockSpec((tm, hcols), lambda t, hh: (t, HEAD_SPLIT + hh)),
            pl.BlockSpec((span, hcols), prev_map_k),
            pl.BlockSpec((span, hcols), prev_map_v),
        ]
        args += [kv, kv, kv, kv]
    return pl.pallas_call(
        functools.partial(_attn_kernel, tiles_per_seq=seq_len // tm),
        grid=(n // tm, HEAD_SPLIT),
        in_specs=in_specs,
        out_specs=pl.BlockSpec((tm, hcols), lambda t, hh: (t, hh)),
        out_shape=jax.ShapeDtypeStruct((n, KV_WIDTH), BF16),
        scratch_shapes=[pltpu.VMEM((N_GROUPS, hcols // V7X_LANES, tm, V7X_LANES), F32)] * 2,
        compiler_params=_params("parallel", "parallel"),
        name="dilated_attention",
    )(*args)


def _out_proj_kernel(x_ref, o_ref_in, w_ref, g_ref, b_ref, out_ref):
    h = jnp.dot(o_ref_in[...], w_ref[...], preferred_element_type=F32)
    out_ref[...] = _layer_norm(ALPHA * x_ref[...] + h, g_ref[...], b_ref[...])


def _out_proj_layer(x2, o, w_o, g, b):
    n = x2.shape[0]
    tm = ROW_TILE
    return pl.pallas_call(
        _out_proj_kernel,
        grid=(n // tm,),
        in_specs=[pl.BlockSpec((tm, D_MODEL), lambda i: (i, 0)),
                  pl.BlockSpec((tm, KV_WIDTH), lambda i: (i, 0)),
                  _resident((KV_WIDTH, D_MODEL)),
                  _resident((1, D_MODEL)),
                  _resident((1, D_MODEL))],
        out_specs=pl.BlockSpec((tm, D_MODEL), lambda i: (i, 0)),
        out_shape=jax.ShapeDtypeStruct((n, D_MODEL), F32),
        compiler_params=_params("parallel"),
        name="attn_out_proj",
    )(x2, o, w_o.astype(BF16), g.reshape(1, -1), b.reshape(1, -1))


def kernel(x, a_w_in, a_ln_g, a_ln_b, a_w_s, a_b_s, a_w_out, kv_w, b_w_q, b_w_o,
           rel_table, ffn_w_up, ffn_conv_w, ffn_conv_b, ffn_w_down, ln_g, ln_b):
    B, T, _ = x.shape
    assert T % SPAN_TILE == 0 and T % ROW_TILE == 0
    assert T % (FFN_SUB_TILE * FFN_SUB_TILES) == 0
    x2 = x.reshape(B * T, D_MODEL)
    blocks_per_half = KV_WIDTH // V7X_MXU_DIM
    dils = [d for _, d in DILATED_GROUPS]
    kv_plan = tuple(tuple((o, c, d) for o, d in enumerate(dils))
                    for c in range(2 * blocks_per_half))
    q_plan = tuple(((g, c, dils[g]),) for g in range(N_GROUPS) for c in range(blocks_per_half))
    bias = kv_groups = None
    for i in range(DEPTH):
        if i < N_A_LAYERS:
            x2 = _mixer_a_layer(x2, a_w_in[i], a_ln_g[i], a_ln_b[i], a_w_s[i], a_b_s[i],
                                a_w_out[i], ln_g[i, 0], ln_b[i, 0])
        else:
            if i == N_A_LAYERS:
                kv_groups = _regroup_proj(x2, kv_w, kv_plan, [2 * KV_WIDTH] * N_GROUPS)
                bias = _band_bias(rel_table)
            j = i - N_A_LAYERS
            q_groups = _regroup_proj(x2, b_w_q[j], q_plan, [KV_WIDTH] * N_GROUPS,
                                     scale=HEAD_DIM ** -0.5 * LOG2E)
            o = _dilated_attention(q_groups, kv_groups, bias, T)
            x2 = _out_proj_layer(x2, o, b_w_o[j], ln_g[i, 0], ln_b[i, 0])
        x2 = _conv_ffn_layer(x2, T, ffn_w_up[i], ffn_conv_w[i], ffn_conv_b[i],
                             ffn_w_down[i], ln_g[i, 1], ln_b[i, 1])
    return x2.reshape(B, T, D_MODEL)
```

```python
import functools
import math

import jax
import jax.numpy as jnp
from jax import lax
from jax.experimental import pallas as pl
from jax.experimental.pallas import tpu as pltpu

D_MODEL = 1024
DEPTH = 4
N_A_LAYERS = DEPTH // 2
CHUNK = 128
SGU_WIDTH = 2 * D_MODEL
SGU_GROUPS = 8
SGU_GROUP_WIDTH = SGU_WIDTH // SGU_GROUPS
HEAD_DIM = 64
KV_HEADS = D_MODEL // 128
KV_WIDTH = KV_HEADS * HEAD_DIM
DILATED_GROUPS = ((128, 1), (512, 4), (2048, 16))
N_GROUPS = len(DILATED_GROUPS)
BLK = 128
REL_BUCKETS = 32
REL_MAX_DIST = 2048
D_FF = 2816
CONV_WIDTH = 3
ALPHA = (2 * DEPTH) ** 0.25
LN_EPS = 1e-5
NEG = -1e30

F32 = jnp.float32
BF16 = jnp.bfloat16

V7X_SUBLANES = 8
V7X_LANES = 128
V7X_MXU_DIM = 256
V7X_VMEM_LIMIT_BYTES = 56 * 1024 * 1024

ROW_TILE = 512
OUT_PROJ_TILE = 1024
WEIGHT_STAGE_ROWS = 128
FFN_SUB_TILE = 512
FFN_SUB_TILES = 1
SPAN_TILE = max(w for w, _ in DILATED_GROUPS)
HEAD_SPLIT = 2
HEAD_COLS = KV_WIDTH // HEAD_SPLIT
HEADS_PER_STEP = KV_HEADS // HEAD_SPLIT
HEADS_PER_LANE_TILE = V7X_LANES // HEAD_DIM
MAX_BLOCK_UNROLL = 16
LOG2E = math.log2(math.e)

assert all(w // d == BLK for w, d in DILATED_GROUPS)


def _layer_norm(x, g, b, eps=LN_EPS):
    mu = jnp.mean(x, axis=-1, keepdims=True)
    xc = x - mu
    var = jnp.mean(xc * xc, axis=-1, keepdims=True)
    return xc * lax.rsqrt(var + eps) * g + b


def _residual_norm(x, h_scaled, g, b):
    return _layer_norm(x + h_scaled, g, b, eps=LN_EPS / (ALPHA * ALPHA))


def _params(*semantics):
    return pltpu.CompilerParams(dimension_semantics=semantics,
                                vmem_limit_bytes=V7X_VMEM_LIMIT_BYTES)


def _resident(shape):
    zeros = (0,) * len(shape)
    return pl.BlockSpec(shape, lambda *_: zeros, pipeline_mode=pl.Buffered(1))


_HBM = pl.BlockSpec(memory_space=pl.ANY)


def _stage_weight(w_hbm, dst_ref, stage_ref, sem_ref, scale):
    chunk = stage_ref.shape[1]
    n_chunks = w_hbm.shape[0] // chunk

    def chunk_copy(k):
        return pltpu.make_async_copy(w_hbm.at[pl.ds(k * chunk, chunk), :],
                                     stage_ref.at[k % 2], sem_ref.at[k % 2])

    chunk_copy(0).start()
    for k in range(n_chunks):
        if k + 1 < n_chunks:
            chunk_copy(k + 1).start()
        chunk_copy(k).wait()
        w = stage_ref[k % 2]
        if scale != 1.0:
            w = w * scale
        dst_ref[k * chunk:(k + 1) * chunk, :] = w.astype(dst_ref.dtype)


def _stage_scratch(rows, cols, chunk):
    assert rows % chunk == 0
    return [pltpu.VMEM((rows, cols), BF16), pltpu.VMEM((2, chunk, cols), F32),
            pltpu.SemaphoreType.DMA((2,))]


def _sgu_kernel(x_ref, win_hbm, lng_ref, lnb_ref, ws_ref, bs_ref, wout_hbm,
                g_ref, b_ref, o_ref, vn_ref, win_ref, win_stage, win_sem,
                wout_ref, wout_stage, wout_sem):
    tm = x_ref.shape[0]

    @pl.when(pl.program_id(0) == 0)
    def _():
        _stage_weight(win_hbm, win_ref, win_stage, win_sem, math.sqrt(0.5))
        _stage_weight(wout_hbm, wout_ref, wout_stage, wout_sem, math.sqrt(0.5) / ALPHA)

    x = x_ref[...]
    xb = x.astype(BF16)
    zv = jnp.dot(xb, win_ref[:, SGU_WIDTH:], preferred_element_type=F32)
    zu = jnp.dot(xb, win_ref[:, :SGU_WIDTH], preferred_element_type=F32)
    v = (math.sqrt(0.5) * zv) * (1.0 + lax.erf(zv))
    vn_ref[...] = _layer_norm(v, lng_ref[...], lnb_ref[...]).astype(BF16)
    u = zu * (1.0 + lax.erf(zu))

    row = lax.broadcasted_iota(jnp.int32, (CHUNK, CHUNK), 0)
    col = lax.broadcasted_iota(jnp.int32, (CHUNK, CHUNK), 1)
    causal = row >= col
    sv = []
    for g in range(SGU_GROUPS):
        cols = slice(g * SGU_GROUP_WIDTH, (g + 1) * SGU_GROUP_WIDTH)
        ws = jnp.where(causal, ws_ref[g], 0.0).astype(BF16)
        sv.append(jnp.concatenate(
            [jnp.dot(ws, vn_ref[c * CHUNK:(c + 1) * CHUNK, cols], preferred_element_type=F32)
             + bs_ref[:, cols] for c in range(tm // CHUNK)], axis=0))
    y = (u * jnp.concatenate(sv, axis=1)).astype(BF16)
    h = jnp.dot(y, wout_ref[...], preferred_element_type=F32)
    o_ref[...] = _residual_norm(x, h, g_ref[...], b_ref[...])


def _mixer_a_layer(x2, w_in, ln_g, ln_b, w_s, b_s, w_out, g, b):
    n = x2.shape[0]
    tm = ROW_TILE
    return pl.pallas_call(
        _sgu_kernel,
        grid=(n // tm,),
        in_specs=[
            pl.BlockSpec((tm, D_MODEL), lambda i: (i, 0)),
            _HBM,
            _resident((1, SGU_WIDTH)),
            _resident((1, SGU_WIDTH)),
            _resident((SGU_GROUPS, CHUNK, CHUNK)),
            _resident((CHUNK, SGU_WIDTH)),
            _HBM,
            _resident((1, D_MODEL)),
            _resident((1, D_MODEL)),
        ],
        out_specs=pl.BlockSpec((tm, D_MODEL), lambda i: (i, 0)),
        out_shape=jax.ShapeDtypeStruct((n, D_MODEL), F32),
        scratch_shapes=([pltpu.VMEM((tm, SGU_WIDTH), BF16)]
                        + _stage_scratch(D_MODEL, 2 * SGU_WIDTH, WEIGHT_STAGE_ROWS)
                        + _stage_scratch(SGU_WIDTH, D_MODEL, WEIGHT_STAGE_ROWS)),
        compiler_params=_params("arbitrary"),
        name="sgu_mixer",
    )(x2, w_in, ln_g.reshape(1, -1), ln_b.reshape(1, -1), w_s,
      jnp.repeat(b_s.T, SGU_GROUP_WIDTH, axis=1), w_out, g.reshape(1, -1), b.reshape(1, -1))


def _ffn_kernel(x_ref, wup_hbm, cw_ref, cb_ref, wdn_hbm, g_ref, b_ref, o_ref,
                xs_ref, ys_ref, carry_ref, wup_ref, wup_stage, wup_sem,
                wdn_ref, wdn_stage, wdn_sem, *, tiles_per_seq):
    @pl.when(pl.program_id(0) == 0)
    def _():
        _stage_weight(wup_hbm, wup_ref, wup_stage, wup_sem, 1.0)
        _stage_weight(wdn_hbm, wdn_ref, wdn_stage, wdn_sem, math.sqrt(0.5) / ALPHA)

    tm = FFN_SUB_TILE
    sub = V7X_SUBLANES
    vrows = tm // sub
    pitch = vrows + sub
    n_slabs = D_MODEL // V7X_LANES
    seq_start = (pl.program_id(0) % tiles_per_seq) == 0
    sublane = lax.broadcasted_iota(jnp.int32, (sub, 2 * D_FF), 0)
    old = jnp.where(seq_start, 0.0, carry_ref[...])

    n_sub_tiles = x_ref.shape[0] // tm
    xbs = []
    for t in range(n_sub_tiles):
        x = x_ref[t * tm:(t + 1) * tm, :]
        for l in range(n_slabs):
            for s in range(sub):
                xs_ref[t, l, s * pitch:s * pitch + vrows, :] = (
                    x[s * vrows:(s + 1) * vrows, l * V7X_LANES:(l + 1) * V7X_LANES])
        xbs.append(jnp.concatenate(
            [jnp.concatenate([xs_ref[t, l, pl.ds(j, sub, stride=pitch), :]
                              for j in range(vrows)], axis=0)
             for l in range(n_slabs)], axis=1).astype(BF16))

    for t in range(n_sub_tiles):
        x = x_ref[t * tm:(t + 1) * tm, :]
        h = jnp.dot(xbs[t], wup_ref[...], preferred_element_type=F32)
        wrap1 = jnp.where(sublane == 0, pltpu.roll(old[sub:], 1, 0),
                          pltpu.roll(h[tm - sub:tm], 1, 0))
        wrap2 = jnp.where(sublane == 0, pltpu.roll(old[:sub], 1, 0),
                          pltpu.roll(h[tm - 2 * sub:tm - sub], 1, 0))
        old = h[tm - 2 * sub:tm]
        back1 = jnp.concatenate([wrap1, h[:tm - sub]], axis=0)
        back2 = jnp.concatenate([wrap2, wrap1, h[:tm - 2 * sub]], axis=0)
        hc = back2 * cw_ref[0:1] + back1 * cw_ref[1:2] + h * cw_ref[2:3] + cb_ref[...]
        a, gate = hc[:, :D_FF], hc[:, D_FF:]
        act = (a * (1.0 + lax.erf(a)) * gate).astype(BF16)
        f_perm = jnp.dot(act, wdn_ref[...], preferred_element_type=F32)

        for l in range(n_slabs):
            ys_ref[t, l] = f_perm[:, l * V7X_LANES:(l + 1) * V7X_LANES]
        f = jnp.concatenate(
            [jnp.concatenate([ys_ref[t, l, pl.ds(s, vrows, stride=sub), :] for s in range(sub)],
                             axis=0) for l in range(n_slabs)], axis=1)
        o_ref[t * tm:(t + 1) * tm, :] = _residual_norm(x, f, g_ref[...], b_ref[...])
    carry_ref[...] = old


def _conv_ffn_layer(x2, seq_len, w_up, conv_w, conv_b, w_down, g, b):
    n = x2.shape[0]
    tm = FFN_SUB_TILE * FFN_SUB_TILES
    n_slabs = D_MODEL // V7X_LANES
    pitch = FFN_SUB_TILE // V7X_SUBLANES + V7X_SUBLANES
    half_scale = jnp.concatenate([jnp.full((D_FF,), math.sqrt(0.5), F32), jnp.ones((D_FF,), F32)])
    return pl.pallas_call(
        functools.partial(_ffn_kernel, tiles_per_seq=seq_len // tm),
        grid=(n // tm,),
        in_specs=[
            pl.BlockSpec((tm, D_MODEL), lambda i: (i, 0)),
            _HBM,
            _resident((CONV_WIDTH, 2 * D_FF)),
            _resident((1, 2 * D_FF)),
            _HBM,
            _resident((1, D_MODEL)),
            _resident((1, D_MODEL)),
        ],
        out_specs=pl.BlockSpec((tm, D_MODEL), lambda i: (i, 0)),
        out_shape=jax.ShapeDtypeStruct((n, D_MODEL), F32),
        scratch_shapes=([
            pltpu.VMEM((FFN_SUB_TILES, n_slabs, V7X_SUBLANES * pitch, V7X_LANES), F32),
            pltpu.VMEM((FFN_SUB_TILES, n_slabs, FFN_SUB_TILE, V7X_LANES), F32),
            pltpu.VMEM((2 * V7X_SUBLANES, 2 * D_FF), F32)]
            + _stage_scratch(D_MODEL, 2 * D_FF, WEIGHT_STAGE_ROWS)
            + _stage_scratch(D_FF, D_MODEL, WEIGHT_STAGE_ROWS)),
        compiler_params=_params("arbitrary"),
        name="conv_ffn",
    )(x2, w_up, conv_w * half_scale, (conv_b * half_scale).reshape(1, -1),
      w_down, g.reshape(1, -1), b.reshape(1, -1))


def _regroup_proj_kernel(x_ref, w_ref, *rest, plan):
    n_out = len({o for routes in plan for (o, _, _) in routes})
    out_refs, res_ref = rest[:n_out], rest[n_out]
    tm = x_ref.shape[0]
    cb = V7X_MXU_DIM
    rc = ROW_TILE
    xbs = [x_ref[r * rc:(r + 1) * rc, :].astype(BF16) for r in range(tm // rc)]

    def project(c):
        slot = c % res_ref.shape[0]
        for r, xb in enumerate(xbs):
            res = jnp.dot(xb, w_ref[:, c * cb:(c + 1) * cb], preferred_element_type=F32)
            for l in range(cb // V7X_LANES):
                res_ref[slot, l, r * rc:(r + 1) * rc, :] = res[:, l * V7X_LANES:(l + 1) * V7X_LANES]
            for (o, oc, d) in plan[c]:
                if d == 1:
                    out_refs[o][r * rc:(r + 1) * rc, oc * cb:(oc + 1) * cb] = (
                        res.astype(out_refs[o].dtype))

    def regroup(c):
        slot = c % res_ref.shape[0]
        for (o, oc, d) in plan[c]:
            out = out_refs[o]
            if d == 1:
                continue
            span = BLK * d
            for s in range(tm // span):
                for r in range(d):
                    rows = jnp.concatenate(
                        [res_ref[slot, l, pl.ds(s * span + r, BLK, stride=d), :]
                         for l in range(cb // V7X_LANES)], axis=-1)
                    out[s * span + r * BLK:s * span + (r + 1) * BLK,
                        oc * cb:(oc + 1) * cb] = rows.astype(out.dtype)

    project(0)
    for c in range(len(plan)):
        if c + 1 < len(plan):
            project(c + 1)
        regroup(c)


def _regroup_proj(x2, w, plan, out_widths, scale=1.0):
    n = x2.shape[0]
    tm = SPAN_TILE
    return pl.pallas_call(
        functools.partial(_regroup_proj_kernel, plan=plan),
        grid=(n // tm,),
        in_specs=[pl.BlockSpec((tm, D_MODEL), lambda i: (i, 0)),
                  _resident(w.shape)],
        out_specs=[pl.BlockSpec((tm, wd), lambda i: (i, 0)) for wd in out_widths],
        out_shape=[jax.ShapeDtypeStruct((n, wd), BF16) for wd in out_widths],
        scratch_shapes=[pltpu.VMEM((2, V7X_MXU_DIM // V7X_LANES, tm, V7X_LANES), F32)],
        compiler_params=_params("parallel"),
        name="regroup_proj",
    )(x2, (w * scale).astype(BF16))


def _bias_kernel(table_ref, idx_ref, o_ref):
    g = pl.program_id(0)
    idx = idx_ref[0]
    row = lax.broadcasted_iota(jnp.int32, idx.shape, 0)
    col = lax.broadcasted_iota(jnp.int32, idx.shape, 1)
    delta = row + BLK - col
    in_band = jnp.logical_and(delta >= 0, delta <= BLK)
    in_band_cur = jnp.logical_and(in_band, col >= BLK)
    for h in range(KV_HEADS):
        acc = jnp.zeros(idx.shape, F32)
        for bkt in range(REL_BUCKETS):
            acc = jnp.where(idx == bkt,
                            table_ref[bkt * (N_GROUPS * KV_HEADS) + g * KV_HEADS + h], acc)
        acc = acc * LOG2E
        o_ref[0, 0, h] = jnp.where(in_band, acc, NEG)
        o_ref[1, 0, h] = jnp.where(in_band_cur, acc, NEG)


def _t5_bucket(dist):
    n = dist.astype(jnp.int32)
    max_exact = REL_BUCKETS // 2
    nf = jnp.maximum(n, 1).astype(F32)
    large = max_exact + (jnp.log(nf / max_exact) / math.log(REL_MAX_DIST / max_exact)
                         * (REL_BUCKETS - max_exact)).astype(jnp.int32)
    large = jnp.minimum(large, REL_BUCKETS - 1)
    return jnp.where(n < max_exact, n, large)


def _band_bias(rel_table):
    iq = jnp.arange(BLK)[:, None]
    ik = jnp.arange(2 * BLK)[None, :]
    delta = jnp.clip(iq + BLK - ik, 0, None)
    idx = jnp.stack([_t5_bucket(delta * d) for _, d in DILATED_GROUPS])
    return pl.pallas_call(
        _bias_kernel,
        grid=(N_GROUPS,),
        in_specs=[pl.BlockSpec(memory_space=pltpu.SMEM),
                  pl.BlockSpec((1, BLK, 2 * BLK), lambda g: (g, 0, 0))],
        out_specs=pl.BlockSpec((2, 1, KV_HEADS, BLK, 2 * BLK), lambda g: (0, g, 0, 0, 0)),
        out_shape=jax.ShapeDtypeStruct((2, N_GROUPS, KV_HEADS, BLK, 2 * BLK), F32),
        compiler_params=_params("parallel"),
        name="band_bias",
    )(rel_table.reshape(-1), idx)


def _attn_kernel(bias_ref, *refs, tiles_per_seq):
    q_refs = refs[0:N_GROUPS]
    kv_refs = refs[N_GROUPS:5 * N_GROUPS]
    o_ref = refs[5 * N_GROUPS]
    og_ref, lse_ref = refs[5 * N_GROUPS + 1:]
    tm = o_ref.shape[0]
    n_blocks = tm // BLK
    seq_start = (pl.program_id(0) % tiles_per_seq) == 0

    q_lane = lax.broadcasted_iota(jnp.int32, (BLK, V7X_LANES), 1) // HEAD_DIM
    kv_lane = lax.broadcasted_iota(jnp.int32, (2 * BLK, V7X_LANES), 1) // HEAD_DIM
    ones_cols = [jnp.where(kv_lane == i, 1.0, 0.0).astype(BF16)
                 for i in range(HEADS_PER_LANE_TILE)]
    nt = (((1,), (1,)), ((), ()))

    def block(blk, carry, *, g, d, first_span):
        q_ref = q_refs[g]
        kc_ref, vc_ref, kp_ref, vp_ref = kv_refs[4 * g:4 * g + 4]
        cur = pl.ds(pl.multiple_of(blk * BLK, BLK), BLK)
        if first_span:
            prev = pl.ds(pl.multiple_of(blk * BLK, BLK), BLK)
            kprev_ref, vprev_ref = kp_ref, vp_ref
            variant = seq_start.astype(jnp.int32)
            start = blk
        else:
            prev = pl.ds(pl.multiple_of((blk - d) * BLK, BLK), BLK)
            kprev_ref, vprev_ref = kc_ref, vc_ref
            variant = 0
            start = (blk // d) * (BLK * d) + blk % d
        rows = pl.ds(start, BLK, stride=d) if d > 1 else cur
        for l in range(HEAD_COLS // V7X_LANES):
            lanes = slice(l * V7X_LANES, (l + 1) * V7X_LANES)
            q = q_ref[cur, lanes]
            k_both = jnp.concatenate([kprev_ref[prev, lanes], kc_ref[cur, lanes]], axis=0)
            v_both = jnp.concatenate([vprev_ref[prev, lanes], vc_ref[cur, lanes]], axis=0)
            ps, vxs = [], []
            m_tile = None
            for i in range(HEADS_PER_LANE_TILE):
                h = l * HEADS_PER_LANE_TILE + i
                qm = jnp.where(q_lane == i, q, jnp.zeros_like(q))
                s = lax.dot_general(qm, k_both, nt, preferred_element_type=F32)
                s = s + bias_ref[variant, g, h]
                m = jnp.max(s, axis=-1, keepdims=True)
                ps.append(jnp.exp2(s - m).astype(BF16))
                vxs.append(jnp.concatenate(
                    [jnp.where(kv_lane == i, v_both, jnp.zeros_like(v_both)),
                     ones_cols[i]], axis=1))
                mb = jnp.broadcast_to(m, (BLK, V7X_LANES))
                m_tile = mb if m_tile is None else jnp.where(q_lane == i, mb, m_tile)
            acc = jnp.dot(jnp.concatenate(ps, axis=1), jnp.concatenate(vxs, axis=0),
                          preferred_element_type=F32)
            den = acc[:, V7X_LANES:]
            og_ref[g, l, rows, :] = acc[:, :V7X_LANES] / den
            lse_ref[g, l, rows, :] = m_tile + jnp.log2(den)
        return carry

    for g, (_, d) in enumerate(DILATED_GROUPS):
        for lo, hi, first_span in ((0, d, True), (d, n_blocks, False)):
            if hi > lo:
                unroll = max(u for u in range(1, MAX_BLOCK_UNROLL + 1) if (hi - lo) % u == 0)
                lax.fori_loop(lo, hi, functools.partial(block, g=g, d=d, first_span=first_span),
                              0, unroll=unroll)

    for l in range(HEAD_COLS // V7X_LANES):
        lse = [lse_ref[g, l] for g in range(N_GROUPS)]
        top = functools.reduce(jnp.maximum, lse)
        e = [jnp.exp2(v - top) for v in lse]
        tot = e[0] + e[1] + e[2]
        mix = e[0] * og_ref[0, l] + e[1] * og_ref[1, l] + e[2] * og_ref[2, l]
        o_ref[:, l * V7X_LANES:(l + 1) * V7X_LANES] = (mix / tot).astype(o_ref.dtype)


def _dilated_attention(q_groups, kv_groups, bias, seq_len):
    n = q_groups[0].shape[0]
    tm = SPAN_TILE
    hcols = HEAD_COLS
    in_specs = [pl.BlockSpec((2, N_GROUPS, HEADS_PER_STEP, BLK, 2 * BLK),
                             lambda t, hh: (0, 0, hh, 0, 0))]
    in_specs += [pl.BlockSpec((tm, hcols), lambda t, hh: (t, hh)) for _ in range(N_GROUPS)]
    args = [bias] + list(q_groups)
    for (_, d), kv in zip(DILATED_GROUPS, kv_groups):
        span = BLK * d
        per_tile = tm // span
        prev_map_k = lambda t, hh, p=per_tile: (jnp.maximum(t * p - 1, 0), hh)
        prev_map_v = lambda t, hh, p=per_tile: (jnp.maximum(t * p - 1, 0), HEAD_SPLIT + hh)
        in_specs += [
            pl.BlockSpec((tm, hcols), lambda t, hh: (t, hh)),
            pl.BlockSpec((tm, hcols), lambda t, hh: (t, HEAD_SPLIT + hh)),
            pl.BlockSpec((span, hcols), prev_map_k),
            pl.BlockSpec((span, hcols), prev_map_v),
        ]
        args += [kv, kv, kv, kv]
    return pl.pallas_call(
        functools.partial(_attn_kernel, tiles_per_seq=seq_len // tm),
        grid=(n // tm, HEAD_SPLIT),
        in_specs=in_specs,
        out_specs=pl.BlockSpec((tm, hcols), lambda t, hh: (t, hh)),
        out_shape=jax.ShapeDtypeStruct((n, KV_WIDTH), BF16),
        scratch_shapes=[pltpu.VMEM((N_GROUPS, hcols // V7X_LANES, tm, V7X_LANES), F32)] * 2,
        compiler_params=_params("parallel", "parallel"),
        name="dilated_attention",
    )(*args)


def _out_proj_kernel(x_ref, o_ref_in, w_ref, g_ref, b_ref, out_ref):
    h = jnp.dot(o_ref_in[...], w_ref[...], preferred_element_type=F32)
    out_ref[...] = _residual_norm(x_ref[...], h, g_ref[...], b_ref[...])


def _out_proj_layer(x2, o, w_o, g, b):
    n = x2.shape[0]
    tm = OUT_PROJ_TILE
    return pl.pallas_call(
        _out_proj_kernel,
        grid=(n // tm,),
        in_specs=[pl.BlockSpec((tm, D_MODEL), lambda i: (i, 0)),
                  pl.BlockSpec((tm, KV_WIDTH), lambda i: (i, 0)),
                  _resident((KV_WIDTH, D_MODEL)),
                  _resident((1, D_MODEL)),
                  _resident((1, D_MODEL))],
        out_specs=pl.BlockSpec((tm, D_MODEL), lambda i: (i, 0)),
        out_shape=jax.ShapeDtypeStruct((n, D_MODEL), F32),
        compiler_params=_params("parallel"),
        name="attn_out_proj",
    )(x2, o, (w_o * (1.0 / ALPHA)).astype(BF16), g.reshape(1, -1), b.reshape(1, -1))


def kernel(x, a_w_in, a_ln_g, a_ln_b, a_w_s, a_b_s, a_w_out, kv_w, b_w_q, b_w_o,
           rel_table, ffn_w_up, ffn_conv_w, ffn_conv_b, ffn_w_down, ln_g, ln_b):
    B, T, _ = x.shape
    assert T % SPAN_TILE == 0 and T % ROW_TILE == 0
    assert T % (FFN_SUB_TILE * FFN_SUB_TILES) == 0
    x2 = x.reshape(B * T, D_MODEL)
    blocks_per_half = KV_WIDTH // V7X_MXU_DIM
    dils = [d for _, d in DILATED_GROUPS]
    kv_plan = tuple(tuple((o, c, d) for o, d in enumerate(dils))
                    for c in range(2 * blocks_per_half))
    q_plan = tuple(((g, c, dils[g]),) for g in range(N_GROUPS) for c in range(blocks_per_half))
    bias = kv_groups = None
    for i in range(DEPTH):
        if i < N_A_LAYERS:
            x2 = _mixer_a_layer(x2, a_w_in[i], a_ln_g[i], a_ln_b[i], a_w_s[i], a_b_s[i],
                                a_w_out[i], ln_g[i, 0], ln_b[i, 0])
        else:
            if i == N_A_LAYERS:
                kv_groups = _regroup_proj(x2, kv_w, kv_plan, [2 * KV_WIDTH] * N_GROUPS)
                bias = _band_bias(rel_table)
            j = i - N_A_LAYERS
            q_groups = _regroup_proj(x2, b_w_q[j], q_plan, [KV_WIDTH] * N_GROUPS,
                                     scale=HEAD_DIM ** -0.5 * LOG2E)
            o = _dilated_attention(q_groups, kv_groups, bias, T)
            x2 = _out_proj_layer(x2, o, b_w_o[j], ln_g[i, 0], ln_b[i, 0])
        x2 = _conv_ffn_layer(x2, T, ffn_w_up[i], ffn_conv_w[i], ffn_conv_b[i],
                             ffn_w_down[i], ln_g[i, 1], ln_b[i, 1])
    return x2.reshape(B, T, D_MODEL)
```

```python
import functools
import math

import jax
import jax.numpy as jnp
from jax import lax
from jax.experimental import pallas as pl
from jax.experimental.pallas import tpu as pltpu

D_MODEL = 1024
DEPTH = 4
N_A_LAYERS = DEPTH // 2
CHUNK = 128
SGU_WIDTH = 2 * D_MODEL
SGU_GROUPS = 8
SGU_GROUP_WIDTH = SGU_WIDTH // SGU_GROUPS
HEAD_DIM = 64
KV_HEADS = D_MODEL // 128
KV_WIDTH = KV_HEADS * HEAD_DIM
DILATED_GROUPS = ((128, 1), (512, 4), (2048, 16))
N_GROUPS = len(DILATED_GROUPS)
BLK = 128
REL_BUCKETS = 32
REL_MAX_DIST = 2048
D_FF = 2816
CONV_WIDTH = 3
ALPHA = (2 * DEPTH) ** 0.25
LN_EPS = 1e-5
NEG = -1e30

F32 = jnp.float32
BF16 = jnp.bfloat16

V7X_SUBLANES = 8
V7X_LANES = 128
V7X_MXU_DIM = 256
V7X_VMEM_LIMIT_BYTES = 56 * 1024 * 1024

ROW_TILE = 512
OUT_PROJ_TILE = 1024
WEIGHT_STAGE_BYTES = 1024 * 1024
WEIGHT_STAGE_SLOTS = 4
FFN_SUB_TILE = 512
FFN_SUB_TILES = 1
SPAN_TILE = max(w for w, _ in DILATED_GROUPS)
HEAD_SPLIT = 2
HEAD_COLS = KV_WIDTH // HEAD_SPLIT
HEADS_PER_STEP = KV_HEADS // HEAD_SPLIT
HEADS_PER_LANE_TILE = V7X_LANES // HEAD_DIM
MAX_BLOCK_UNROLL = 16
LOG2E = math.log2(math.e)

assert all(w // d == BLK for w, d in DILATED_GROUPS)


def _layer_norm(x, g, b, eps=LN_EPS):
    mu = jnp.mean(x, axis=-1, keepdims=True)
    xc = x - mu
    var = jnp.mean(xc * xc, axis=-1, keepdims=True)
    return xc * lax.rsqrt(var + eps) * g + b


def _residual_norm(x, h_scaled, g, b):
    return _layer_norm(x + h_scaled, g, b, eps=LN_EPS / (ALPHA * ALPHA))


def _params(*semantics):
    return pltpu.CompilerParams(dimension_semantics=semantics,
                                vmem_limit_bytes=V7X_VMEM_LIMIT_BYTES)


def _resident(shape):
    zeros = (0,) * len(shape)
    return pl.BlockSpec(shape, lambda *_: zeros, pipeline_mode=pl.Buffered(1))


_HBM = pl.BlockSpec(memory_space=pl.ANY)


def _stage_weight(w_hbm, dst_ref, stage_ref, sem_ref, scale):
    slots, chunk = stage_ref.shape[0], stage_ref.shape[1]
    n_chunks = w_hbm.shape[0] // chunk

    def chunk_copy(k):
        return pltpu.make_async_copy(w_hbm.at[pl.ds(k * chunk, chunk), :],
                                     stage_ref.at[k % slots], sem_ref.at[k % slots])

    for k in range(min(slots - 1, n_chunks)):
        chunk_copy(k).start()
    for k in range(n_chunks):
        if k + slots - 1 < n_chunks:
            chunk_copy(k + slots - 1).start()
        chunk_copy(k).wait()
        w = stage_ref[k % slots]
        if scale != 1.0:
            w = w * scale
        dst_ref[k * chunk:(k + 1) * chunk, :] = w.astype(dst_ref.dtype)


def _stage_scratch(rows, cols):
    chunk = max(c for c in range(V7X_SUBLANES, rows + 1, V7X_SUBLANES)
                if rows % c == 0 and c * cols * 4 <= WEIGHT_STAGE_BYTES)
    return [pltpu.VMEM((rows, cols), BF16),
            pltpu.VMEM((WEIGHT_STAGE_SLOTS, chunk, cols), F32),
            pltpu.SemaphoreType.DMA((WEIGHT_STAGE_SLOTS,))]


def _sgu_kernel(x_ref, win_hbm, lng_ref, lnb_ref, ws_ref, bs_ref, wout_hbm,
                g_ref, b_ref, o_ref, vn_ref, win_ref, win_stage, win_sem,
                wout_ref, wout_stage, wout_sem, *, layer):
    tm = x_ref.shape[0]

    @pl.when(pl.program_id(0) == 0)
    def _():
        _stage_weight(win_hbm.at[layer], win_ref, win_stage, win_sem, math.sqrt(0.5))
        _stage_weight(wout_hbm.at[layer], wout_ref, wout_stage, wout_sem,
                      math.sqrt(0.5) / ALPHA)

    x = x_ref[...]
    xb = x.astype(BF16)
    zv = jnp.dot(xb, win_ref[:, SGU_WIDTH:], preferred_element_type=F32)
    zu = jnp.dot(xb, win_ref[:, :SGU_WIDTH], preferred_element_type=F32)
    v = (math.sqrt(0.5) * zv) * (1.0 + lax.erf(zv))
    vn_ref[...] = _layer_norm(v, lng_ref[...], lnb_ref[...]).astype(BF16)
    u = zu * (1.0 + lax.erf(zu))

    row = lax.broadcasted_iota(jnp.int32, (CHUNK, CHUNK), 0)
    col = lax.broadcasted_iota(jnp.int32, (CHUNK, CHUNK), 1)
    causal = row >= col
    sv = []
    for g in range(SGU_GROUPS):
        cols = slice(g * SGU_GROUP_WIDTH, (g + 1) * SGU_GROUP_WIDTH)
        ws = jnp.where(causal, ws_ref[g], 0.0).astype(BF16)
        sv.append(jnp.concatenate(
            [jnp.dot(ws, vn_ref[c * CHUNK:(c + 1) * CHUNK, cols], preferred_element_type=F32)
             + bs_ref[:, cols] for c in range(tm // CHUNK)], axis=0))
    y = (u * jnp.concatenate(sv, axis=1)).astype(BF16)
    h = jnp.dot(y, wout_ref[...], preferred_element_type=F32)
    o_ref[...] = _residual_norm(x, h, g_ref[...], b_ref[...])


def _mixer_a_layer(x2, layer, w_in, ln_g, ln_b, w_s, b_s, w_out, g, b):
    n = x2.shape[0]
    tm = ROW_TILE
    return pl.pallas_call(
        functools.partial(_sgu_kernel, layer=layer),
        grid=(n // tm,),
        in_specs=[
            pl.BlockSpec((tm, D_MODEL), lambda i: (i, 0)),
            _HBM,
            _resident((1, SGU_WIDTH)),
            _resident((1, SGU_WIDTH)),
            _resident((SGU_GROUPS, CHUNK, CHUNK)),
            _resident((CHUNK, SGU_WIDTH)),
            _HBM,
            _resident((1, D_MODEL)),
            _resident((1, D_MODEL)),
        ],
        out_specs=pl.BlockSpec((tm, D_MODEL), lambda i: (i, 0)),
        out_shape=jax.ShapeDtypeStruct((n, D_MODEL), F32),
        scratch_shapes=([pltpu.VMEM((tm, SGU_WIDTH), BF16)]
                        + _stage_scratch(D_MODEL, 2 * SGU_WIDTH)
                        + _stage_scratch(SGU_WIDTH, D_MODEL)),
        compiler_params=_params("arbitrary"),
        name="sgu_mixer",
    )(x2, w_in, ln_g.reshape(1, -1), ln_b.reshape(1, -1), w_s,
      jnp.repeat(b_s.T, SGU_GROUP_WIDTH, axis=1), w_out, g.reshape(1, -1), b.reshape(1, -1))


def _ffn_kernel(x_ref, wup_hbm, cw_ref, cb_ref, wdn_hbm, g_ref, b_ref, o_ref,
                xs_ref, ys_ref, carry_ref, wup_ref, wup_stage, wup_sem,
                wdn_ref, wdn_stage, wdn_sem, *, tiles_per_seq, layer):
    @pl.when(pl.program_id(0) == 0)
    def _():
        _stage_weight(wup_hbm.at[layer], wup_ref, wup_stage, wup_sem, 1.0)
        _stage_weight(wdn_hbm.at[layer], wdn_ref, wdn_stage, wdn_sem, math.sqrt(0.5) / ALPHA)

    tm = FFN_SUB_TILE
    sub = V7X_SUBLANES
    vrows = tm // sub
    pitch = vrows + sub
    n_slabs = D_MODEL // V7X_LANES
    seq_start = (pl.program_id(0) % tiles_per_seq) == 0
    sublane = lax.broadcasted_iota(jnp.int32, (sub, 2 * D_FF), 0)
    old = jnp.where(seq_start, 0.0, carry_ref[...])

    n_sub_tiles = x_ref.shape[0] // tm
    xbs = []
    for t in range(n_sub_tiles):
        x = x_ref[t * tm:(t + 1) * tm, :]
        for l in range(n_slabs):
            for s in range(sub):
                xs_ref[t, l, s * pitch:s * pitch + vrows, :] = (
                    x[s * vrows:(s + 1) * vrows, l * V7X_LANES:(l + 1) * V7X_LANES])
        xbs.append(jnp.concatenate(
            [jnp.concatenate([xs_ref[t, l, pl.ds(j, sub, stride=pitch), :]
                              for j in range(vrows)], axis=0)
             for l in range(n_slabs)], axis=1).astype(BF16))

    for t in range(n_sub_tiles):
        x = x_ref[t * tm:(t + 1) * tm, :]
        h = jnp.dot(xbs[t], wup_ref[...], preferred_element_type=F32)
        wrap1 = jnp.where(sublane == 0, pltpu.roll(old[sub:], 1, 0),
                          pltpu.roll(h[tm - sub:tm], 1, 0))
        wrap2 = jnp.where(sublane == 0, pltpu.roll(old[:sub], 1, 0),
                          pltpu.roll(h[tm - 2 * sub:tm - sub], 1, 0))
        old = h[tm - 2 * sub:tm]
        back1 = jnp.concatenate([wrap1, h[:tm - sub]], axis=0)
        back2 = jnp.concatenate([wrap2, wrap1, h[:tm - 2 * sub]], axis=0)
        hc = back2 * cw_ref[0:1] + back1 * cw_ref[1:2] + h * cw_ref[2:3] + cb_ref[...]
        a, gate = hc[:, :D_FF], hc[:, D_FF:]
        act = (a * (1.0 + lax.erf(a)) * gate).astype(BF16)
        f_perm = jnp.dot(act, wdn_ref[...], preferred_element_type=F32)

        for l in range(n_slabs):
            ys_ref[t, l] = f_perm[:, l * V7X_LANES:(l + 1) * V7X_LANES]
        f = jnp.concatenate(
            [jnp.concatenate([ys_ref[t, l, pl.ds(s, vrows, stride=sub), :] for s in range(sub)],
                             axis=0) for l in range(n_slabs)], axis=1)
        o_ref[t * tm:(t + 1) * tm, :] = _residual_norm(x, f, g_ref[...], b_ref[...])
    carry_ref[...] = old


def _conv_ffn_layer(x2, seq_len, layer, w_up, conv_w, conv_b, w_down, g, b):
    n = x2.shape[0]
    tm = FFN_SUB_TILE * FFN_SUB_TILES
    n_slabs = D_MODEL // V7X_LANES
    pitch = FFN_SUB_TILE // V7X_SUBLANES + V7X_SUBLANES
    half_scale = jnp.concatenate([jnp.full((D_FF,), math.sqrt(0.5), F32), jnp.ones((D_FF,), F32)])
    return pl.pallas_call(
        functools.partial(_ffn_kernel, tiles_per_seq=seq_len // tm, layer=layer),
        grid=(n // tm,),
        in_specs=[
            pl.BlockSpec((tm, D_MODEL), lambda i: (i, 0)),
            _HBM,
            _resident((CONV_WIDTH, 2 * D_FF)),
            _resident((1, 2 * D_FF)),
            _HBM,
            _resident((1, D_MODEL)),
            _resident((1, D_MODEL)),
        ],
        out_specs=pl.BlockSpec((tm, D_MODEL), lambda i: (i, 0)),
        out_shape=jax.ShapeDtypeStruct((n, D_MODEL), F32),
        scratch_shapes=([
            pltpu.VMEM((FFN_SUB_TILES, n_slabs, V7X_SUBLANES * pitch, V7X_LANES), F32),
            pltpu.VMEM((FFN_SUB_TILES, n_slabs, FFN_SUB_TILE, V7X_LANES), F32),
            pltpu.VMEM((2 * V7X_SUBLANES, 2 * D_FF), F32)]
            + _stage_scratch(D_MODEL, 2 * D_FF)
            + _stage_scratch(D_FF, D_MODEL)),
        compiler_params=_params("arbitrary"),
        name="conv_ffn",
    )(x2, w_up, conv_w * half_scale, (conv_b * half_scale).reshape(1, -1),
      w_down, g.reshape(1, -1), b.reshape(1, -1))


def _regroup_proj_kernel(x_ref, w_ref, *rest, plan):
    n_out = len({o for routes in plan for (o, _, _) in routes})
    out_refs, res_ref = rest[:n_out], rest[n_out]
    tm = x_ref.shape[0]
    cb = V7X_MXU_DIM
    rc = ROW_TILE
    xbs = [x_ref[r * rc:(r + 1) * rc, :].astype(BF16) for r in range(tm // rc)]

    def project(c):
        slot = c % res_ref.shape[0]
        for r, xb in enumerate(xbs):
            res = jnp.dot(xb, w_ref[:, c * cb:(c + 1) * cb], preferred_element_type=F32)
            for l in range(cb // V7X_LANES):
                res_ref[slot, l, r * rc:(r + 1) * rc, :] = res[:, l * V7X_LANES:(l + 1) * V7X_LANES]
            for (o, oc, d) in plan[c]:
                if d == 1:
                    out_refs[o][r * rc:(r + 1) * rc, oc * cb:(oc + 1) * cb] = (
                        res.astype(out_refs[o].dtype))

    def regroup(c):
        slot = c % res_ref.shape[0]
        for (o, oc, d) in plan[c]:
            out = out_refs[o]
            if d == 1:
                continue
            span = BLK * d
            for s in range(tm // span):
                for r in range(d):
                    rows = jnp.concatenate(
                        [res_ref[slot, l, pl.ds(s * span + r, BLK, stride=d), :]
                         for l in range(cb // V7X_LANES)], axis=-1)
                    out[s * span + r * BLK:s * span + (r + 1) * BLK,
                        oc * cb:(oc + 1) * cb] = rows.astype(out.dtype)

    project(0)
    for c in range(len(plan)):
        if c + 1 < len(plan):
            project(c + 1)
        regroup(c)


def _regroup_proj(x2, w, plan, out_widths, scale=1.0):
    n = x2.shape[0]
    tm = SPAN_TILE
    return pl.pallas_call(
        functools.partial(_regroup_proj_kernel, plan=plan),
        grid=(n // tm,),
        in_specs=[pl.BlockSpec((tm, D_MODEL), lambda i: (i, 0)),
                  _resident(w.shape)],
        out_specs=[pl.BlockSpec((tm, wd), lambda i: (i, 0)) for wd in out_widths],
        out_shape=[jax.ShapeDtypeStruct((n, wd), BF16) for wd in out_widths],
        scratch_shapes=[pltpu.VMEM((2, V7X_MXU_DIM // V7X_LANES, tm, V7X_LANES), F32)],
        compiler_params=_params("parallel"),
        name="regroup_proj",
    )(x2, (w * scale).astype(BF16))


def _bias_kernel(table_ref, idx_ref, o_ref):
    g = pl.program_id(0)
    idx = idx_ref[0]
    row = lax.broadcasted_iota(jnp.int32, idx.shape, 0)
    col = lax.broadcasted_iota(jnp.int32, idx.shape, 1)
    delta = row + BLK - col
    in_band = jnp.logical_and(delta >= 0, delta <= BLK)
    in_band_cur = jnp.logical_and(in_band, col >= BLK)
    for h in range(KV_HEADS):
        acc = jnp.zeros(idx.shape, F32)
        for bkt in range(REL_BUCKETS):
            acc = jnp.where(idx == bkt,
                            table_ref[bkt * (N_GROUPS * KV_HEADS) + g * KV_HEADS + h], acc)
        acc = acc * LOG2E
        o_ref[0, 0, h] = jnp.where(in_band, acc, NEG)
        o_ref[1, 0, h] = jnp.where(in_band_cur, acc, NEG)


def _t5_bucket(dist):
    n = dist.astype(jnp.int32)
    max_exact = REL_BUCKETS // 2
    nf = jnp.maximum(n, 1).astype(F32)
    large = max_exact + (jnp.log(nf / max_exact) / math.log(REL_MAX_DIST / max_exact)
                         * (REL_BUCKETS - max_exact)).astype(jnp.int32)
    large = jnp.minimum(large, REL_BUCKETS - 1)
    return jnp.where(n < max_exact, n, large)


def _band_bias(rel_table):
    iq = jnp.arange(BLK)[:, None]
    ik = jnp.arange(2 * BLK)[None, :]
    delta = jnp.clip(iq + BLK - ik, 0, None)
    idx = jnp.stack([_t5_bucket(delta * d) for _, d in DILATED_GROUPS])
    return pl.pallas_call(
        _bias_kernel,
        grid=(N_GROUPS,),
        in_specs=[pl.BlockSpec(memory_space=pltpu.SMEM),
                  pl.BlockSpec((1, BLK, 2 * BLK), lambda g: (g, 0, 0))],
        out_specs=pl.BlockSpec((2, 1, KV_HEADS, BLK, 2 * BLK), lambda g: (0, g, 0, 0, 0)),
        out_shape=jax.ShapeDtypeStruct((2, N_GROUPS, KV_HEADS, BLK, 2 * BLK), F32),
        compiler_params=_params("parallel"),
        name="band_bias",
    )(rel_table.reshape(-1), idx)


def _attn_kernel(bias_ref, *refs, tiles_per_seq):
    q_refs = refs[0:N_GROUPS]
    kv_refs = refs[N_GROUPS:5 * N_GROUPS]
    o_ref = refs[5 * N_GROUPS]
    og_ref, lse_ref = refs[5 * N_GROUPS + 1:]
    tm = o_ref.shape[0]
    n_blocks = tm // BLK
    seq_start = (pl.program_id(0) % tiles_per_seq) == 0

    q_lane = lax.broadcasted_iota(jnp.int32, (BLK, V7X_LANES), 1) // HEAD_DIM
    kv_lane = lax.broadcasted_iota(jnp.int32, (2 * BLK, V7X_LANES), 1) // HEAD_DIM
    ones_cols = [jnp.where(kv_lane == i, 1.0, 0.0).astype(BF16)
                 for i in range(HEADS_PER_LANE_TILE)]
    nt = (((1,), (1,)), ((), ()))

    def block(blk, carry, *, g, d, first_span):
        q_ref = q_refs[g]
        kc_ref, vc_ref, kp_ref, vp_ref = kv_refs[4 * g:4 * g + 4]
        cur = pl.ds(pl.multiple_of(blk * BLK, BLK), BLK)
        if first_span:
            prev = pl.ds(pl.multiple_of(blk * BLK, BLK), BLK)
            kprev_ref, vprev_ref = kp_ref, vp_ref
            variant = seq_start.astype(jnp.int32)
            start = blk
        else:
            prev = pl.ds(pl.multiple_of((blk - d) * BLK, BLK), BLK)
            kprev_ref, vprev_ref = kc_ref, vc_ref
            variant = 0
            start = (blk // d) * (BLK * d) + blk % d
        rows = pl.ds(start, BLK, stride=d) if d > 1 else cur
        for l in range(HEAD_COLS // V7X_LANES):
            lanes = slice(l * V7X_LANES, (l + 1) * V7X_LANES)
            q = q_ref[cur, lanes]
            k_both = jnp.concatenate([kprev_ref[prev, lanes], kc_ref[cur, lanes]], axis=0)
            v_both = jnp.concatenate([vprev_ref[prev, lanes], vc_ref[cur, lanes]], axis=0)
            ps, vxs = [], []
            m_tile = None
            for i in range(HEADS_PER_LANE_TILE):
                h = l * HEADS_PER_LANE_TILE + i
                qm = jnp.where(q_lane == i, q, jnp.zeros_like(q))
                s = lax.dot_general(qm, k_both, nt, preferred_element_type=F32)
                s = s + bias_ref[variant, g, h]
                m = jnp.max(s, axis=-1, keepdims=True)
                ps.append(jnp.exp2(s - m).astype(BF16))
                vxs.append(jnp.concatenate(
                    [jnp.where(kv_lane == i, v_both, jnp.zeros_like(v_both)),
                     ones_cols[i]], axis=1))
                mb = jnp.broadcast_to(m, (BLK, V7X_LANES))
                m_tile = mb if m_tile is None else jnp.where(q_lane == i, mb, m_tile)
            acc = jnp.dot(jnp.concatenate(ps, axis=1), jnp.concatenate(vxs, axis=0),
                          preferred_element_type=F32)
            den = acc[:, V7X_LANES:]
            og_ref[g, l, rows, :] = acc[:, :V7X_LANES] / den
            lse_ref[g, l, rows, :] = m_tile + jnp.log2(den)
        return carry

    for g, (_, d) in enumerate(DILATED_GROUPS):
        for lo, hi, first_span in ((0, d, True), (d, n_blocks, False)):
            if hi > lo:
                unroll = max(u for u in range(1, MAX_BLOCK_UNROLL + 1) if (hi - lo) % u == 0)
                lax.fori_loop(lo, hi, functools.partial(block, g=g, d=d, first_span=first_span),
                              0, unroll=unroll)

    for l in range(HEAD_COLS // V7X_LANES):
        lse = [lse_ref[g, l] for g in range(N_GROUPS)]
        top = functools.reduce(jnp.maximum, lse)
        e = [jnp.exp2(v - top) for v in lse]
        tot = e[0] + e[1] + e[2]
        mix = e[0] * og_ref[0, l] + e[1] * og_ref[1, l] + e[2] * og_ref[2, l]
        o_ref[:, l * V7X_LANES:(l + 1) * V7X_LANES] = (mix / tot).astype(o_ref.dtype)


def _dilated_attention(q_groups, kv_groups, bias, seq_len):
    n = q_groups[0].shape[0]
    tm = SPAN_TILE
    hcols = HEAD_COLS
    in_specs = [pl.BlockSpec((2, N_GROUPS, HEADS_PER_STEP, BLK, 2 * BLK),
                             lambda t, hh: (0, 0, hh, 0, 0))]
    in_specs += [pl.BlockSpec((tm, hcols), lambda t, hh: (t, hh)) for _ in range(N_GROUPS)]
    args = [bias] + list(q_groups)
    for (_, d), kv in zip(DILATED_GROUPS, kv_groups):
        span = BLK * d
        per_tile = tm // span
        prev_map_k = lambda t, hh, p=per_tile: (jnp.maximum(t * p - 1, 0), hh)
        prev_map_v = lambda t, hh, p=per_tile: (jnp.maximum(t * p - 1, 0), HEAD_SPLIT + hh)
        in_specs += [
            pl.BlockSpec((tm, hcols), lambda t, hh: (t, hh)),
            pl.BlockSpec((tm, hcols), lambda t, hh: (t, HEAD_SPLIT + hh)),
            pl.BlockSpec((span, hcols), prev_map_k),
            pl.BlockSpec((span, hcols), prev_map_v),
        ]
        args += [kv, kv, kv, kv]
    return pl.pallas_call(
        functools.partial(_attn_kernel, tiles_per_seq=seq_len // tm),
        grid=(n // tm, HEAD_SPLIT),
        in_specs=in_specs,
        out_specs=pl.BlockSpec((tm, hcols), lambda t, hh: (t, hh)),
        out_shape=jax.ShapeDtypeStruct((n, KV_WIDTH), BF16),
        scratch_shapes=[pltpu.VMEM((N_GROUPS, hcols // V7X_LANES, tm, V7X_LANES), F32)] * 2,
        compiler_params=_params("parallel", "parallel"),
        name="dilated_attention",
    )(*args)


def _out_proj_kernel(x_ref, o_ref_in, w_ref, g_ref, b_ref, out_ref):
    h = jnp.dot(o_ref_in[...], w_ref[...], preferred_element_type=F32)
    out_ref[...] = _residual_norm(x_ref[...], h, g_ref[...], b_ref[...])


def _out_proj_layer(x2, o, w_o, g, b):
    n = x2.shape[0]
    tm = OUT_PROJ_TILE
    return pl.pallas_call(
        _out_proj_kernel,
        grid=(n // tm,),
        in_specs=[pl.BlockSpec((tm, D_MODEL), lambda i: (i, 0)),
                  pl.BlockSpec((tm, KV_WIDTH), lambda i: (i, 0)),
                  _resident((KV_WIDTH, D_MODEL)),
                  _resident((1, D_MODEL)),
                  _resident((1, D_MODEL))],
        out_specs=pl.BlockSpec((tm, D_MODEL), lambda i: (i, 0)),
        out_shape=jax.ShapeDtypeStruct((n, D_MODEL), F32),
        compiler_params=_params("parallel"),
        name="attn_out_proj",
    )(x2, o, (w_o * (1.0 / ALPHA)).astype(BF16), g.reshape(1, -1), b.reshape(1, -1))


def kernel(x, a_w_in, a_ln_g, a_ln_b, a_w_s, a_b_s, a_w_out, kv_w, b_w_q, b_w_o,
           rel_table, ffn_w_up, ffn_conv_w, ffn_conv_b, ffn_w_down, ln_g, ln_b):
    B, T, _ = x.shape
    assert T % SPAN_TILE == 0 and T % ROW_TILE == 0
    assert T % (FFN_SUB_TILE * FFN_SUB_TILES) == 0
    x2 = x.reshape(B * T, D_MODEL)
    blocks_per_half = KV_WIDTH // V7X_MXU_DIM
    dils = [d for _, d in DILATED_GROUPS]
    kv_plan = tuple(tuple((o, c, d) for o, d in enumerate(dils))
                    for c in range(2 * blocks_per_half))
    q_plan = tuple(((g, c, dils[g]),) for g in range(N_GROUPS) for c in range(blocks_per_half))
    bias = kv_groups = None
    for i in range(DEPTH):
        if i < N_A_LAYERS:
            x2 = _mixer_a_layer(x2, i, a_w_in, a_ln_g[i], a_ln_b[i], a_w_s[i], a_b_s[i],
                                a_w_out, ln_g[i, 0], ln_b[i, 0])
        else:
            if i == N_A_LAYERS:
                kv_groups = _regroup_proj(x2, kv_w, kv_plan, [2 * KV_WIDTH] * N_GROUPS)
                bias = _band_bias(rel_table)
            j = i - N_A_LAYERS
            q_groups = _regroup_proj(x2, b_w_q[j], q_plan, [KV_WIDTH] * N_GROUPS,
                                     scale=HEAD_DIM ** -0.5 * LOG2E)
            o = _dilated_attention(q_groups, kv_groups, bias, T)
            x2 = _out_proj_layer(x2, o, b_w_o[j], ln_g[i, 0], ln_b[i, 0])
        x2 = _conv_ffn_layer(x2, T, i, ffn_w_up, ffn_conv_w[i], ffn_conv_b[i],
                             ffn_w_down, ln_g[i, 1], ln_b[i, 1])
    return x2.reshape(B, T, D_MODEL)
```

```python
import functools
import math

import numpy as np
import jax
import jax.numpy as jnp
from jax import lax
from jax.experimental import pallas as pl
from jax.experimental.pallas import tpu as pltpu

D_MODEL = 1024
DEPTH = 4
N_A_LAYERS = DEPTH // 2
CHUNK = 128
SGU_WIDTH = 2 * D_MODEL
SGU_GROUPS = 8
SGU_GROUP_WIDTH = SGU_WIDTH // SGU_GROUPS
HEAD_DIM = 64
KV_HEADS = D_MODEL // 128
KV_WIDTH = KV_HEADS * HEAD_DIM
DILATED_GROUPS = ((128, 1), (512, 4), (2048, 16))
N_GROUPS = len(DILATED_GROUPS)
BLK = 128
REL_BUCKETS = 32
REL_MAX_DIST = 2048
D_FF = 2816
CONV_WIDTH = 3
ALPHA = (2 * DEPTH) ** 0.25
LN_EPS = 1e-5
NEG = -1e30

F32 = jnp.float32
BF16 = jnp.bfloat16

V7X_SUBLANES = 8
V7X_LANES = 128
V7X_MXU_DIM = 256
V7X_VMEM_LIMIT_BYTES = 56 * 1024 * 1024

ROW_TILE = 512
OUT_PROJ_TILE = 1024
WEIGHT_STAGE_BYTES = 1024 * 1024
WEIGHT_STAGE_SLOTS = 4
FFN_SUB_TILE = 512
FFN_SUB_TILES = 1
FFN_DOWN_PARTS = 2
SGU_OUT_PARTS = 2
SPAN_TILE = max(w for w, _ in DILATED_GROUPS)
HEAD_SPLIT = 2
HEAD_COLS = KV_WIDTH // HEAD_SPLIT
HEADS_PER_STEP = KV_HEADS // HEAD_SPLIT
HEADS_PER_LANE_TILE = V7X_LANES // HEAD_DIM
LOG2E = math.log2(math.e)

assert all(w // d == BLK for w, d in DILATED_GROUPS)


def _layer_norm(x, g, b, eps=LN_EPS):
    mu = jnp.mean(x, axis=-1, keepdims=True)
    xc = x - mu
    var = jnp.mean(xc * xc, axis=-1, keepdims=True)
    return xc * lax.rsqrt(var + eps) * g + b


def _residual_norm(x, h_scaled, g, b):
    return _layer_norm(x + h_scaled, g, b, eps=LN_EPS / (ALPHA * ALPHA))


def _params(*semantics):
    return pltpu.CompilerParams(dimension_semantics=semantics,
                                vmem_limit_bytes=V7X_VMEM_LIMIT_BYTES)


def _resident(shape):
    zeros = (0,) * len(shape)
    return pl.BlockSpec(shape, lambda *_: zeros, pipeline_mode=pl.Buffered(1))


_HBM = pl.BlockSpec(memory_space=pl.ANY)


def _stage_weight(w_hbm, dst_ref, stage_ref, sem_ref, scale):
    slots, chunk = stage_ref.shape[0], stage_ref.shape[1]
    n_chunks = w_hbm.shape[0] // chunk

    def chunk_copy(k):
        return pltpu.make_async_copy(w_hbm.at[pl.ds(k * chunk, chunk), :],
                                     stage_ref.at[k % slots], sem_ref.at[k % slots])

    for k in range(min(slots - 1, n_chunks)):
        chunk_copy(k).start()
    for k in range(n_chunks):
        if k + slots - 1 < n_chunks:
            chunk_copy(k + slots - 1).start()
        chunk_copy(k).wait()
        w = stage_ref[k % slots]
        if scale != 1.0:
            w = w * scale
        dst_ref[k * chunk:(k + 1) * chunk, :] = w.astype(dst_ref.dtype)


def _stage_scratch(rows, cols):
    chunk = max(c for c in range(V7X_SUBLANES, rows + 1, V7X_SUBLANES)
                if rows % c == 0 and c * cols * 4 <= WEIGHT_STAGE_BYTES)
    return [pltpu.VMEM((rows, cols), BF16),
            pltpu.VMEM((WEIGHT_STAGE_SLOTS, chunk, cols), F32),
            pltpu.SemaphoreType.DMA((WEIGHT_STAGE_SLOTS,))]


def _sgu_kernel(x_ref, win_hbm, lng_ref, lnb_ref, ws_ref, bs_ref, wout_hbm,
                g_ref, b_ref, o_ref, vn_ref, win_ref, win_stage, win_sem,
                wout_ref, wout_stage, wout_sem, *, layer):
    tm = x_ref.shape[0]

    @pl.when(pl.program_id(0) == 0)
    def _():
        _stage_weight(win_hbm.at[layer], win_ref, win_stage, win_sem, math.sqrt(0.5))
        _stage_weight(wout_hbm.at[layer], wout_ref, wout_stage, wout_sem,
                      math.sqrt(0.5) / ALPHA)

    x = x_ref[...]
    xb = x.astype(BF16)
    zv = jnp.dot(xb, win_ref[:, SGU_WIDTH:], preferred_element_type=F32)
    zu = jnp.dot(xb, win_ref[:, :SGU_WIDTH], preferred_element_type=F32)
    v = (math.sqrt(0.5) * zv) * (1.0 + lax.erf(zv))
    vn_ref[...] = _layer_norm(v, lng_ref[...], lnb_ref[...]).astype(BF16)
    u = zu * (1.0 + lax.erf(zu))

    row = lax.broadcasted_iota(jnp.int32, (CHUNK, CHUNK), 0)
    col = lax.broadcasted_iota(jnp.int32, (CHUNK, CHUNK), 1)
    causal = row >= col
    sv = []
    for g in range(SGU_GROUPS):
        cols = slice(g * SGU_GROUP_WIDTH, (g + 1) * SGU_GROUP_WIDTH)
        ws = jnp.where(causal, ws_ref[g], 0.0).astype(BF16)
        sv.append(jnp.concatenate(
            [jnp.dot(ws, vn_ref[c * CHUNK:(c + 1) * CHUNK, cols], preferred_element_type=F32)
             + bs_ref[:, cols] for c in range(tm // CHUNK)], axis=0))
    y = (u * jnp.concatenate(sv, axis=1)).astype(BF16)
    rp = tm // SGU_OUT_PARTS
    for p in range(SGU_OUT_PARTS):
        rows = slice(p * rp, (p + 1) * rp)
        h = jnp.dot(y[rows], wout_ref[...], preferred_element_type=F32)
        o_ref[rows, :] = _residual_norm(x[rows], h, g_ref[...], b_ref[...])


def _mixer_a_layer(x2, layer, w_in, ln_g, ln_b, w_s, b_s, w_out, g, b):
    n = x2.shape[0]
    tm = ROW_TILE
    return pl.pallas_call(
        functools.partial(_sgu_kernel, layer=layer),
        grid=(n // tm,),
        in_specs=[
            pl.BlockSpec((tm, D_MODEL), lambda i: (i, 0)),
            _HBM,
            _resident((1, SGU_WIDTH)),
            _resident((1, SGU_WIDTH)),
            _resident((SGU_GROUPS, CHUNK, CHUNK)),
            _resident((CHUNK, SGU_WIDTH)),
            _HBM,
            _resident((1, D_MODEL)),
            _resident((1, D_MODEL)),
        ],
        out_specs=pl.BlockSpec((tm, D_MODEL), lambda i: (i, 0)),
        out_shape=jax.ShapeDtypeStruct((n, D_MODEL), F32),
        scratch_shapes=([pltpu.VMEM((tm, SGU_WIDTH), BF16)]
                        + _stage_scratch(D_MODEL, 2 * SGU_WIDTH)
                        + _stage_scratch(SGU_WIDTH, D_MODEL)),
        compiler_params=_params("arbitrary"),
        name="sgu_mixer",
    )(x2, w_in, ln_g.reshape(1, -1), ln_b.reshape(1, -1), w_s,
      jnp.repeat(b_s.T, SGU_GROUP_WIDTH, axis=1), w_out, g.reshape(1, -1), b.reshape(1, -1))


def _ffn_kernel(x_ref, wup_hbm, cw_ref, cb_ref, wdn_hbm, g_ref, b_ref, o_ref,
                xs_ref, ys_ref, carry_ref, wup_ref, wup_stage, wup_sem,
                wdn_ref, wdn_stage, wdn_sem, *, tiles_per_seq, layer):
    @pl.when(pl.program_id(0) == 0)
    def _():
        _stage_weight(wup_hbm.at[layer], wup_ref, wup_stage, wup_sem, 1.0)
        _stage_weight(wdn_hbm.at[layer], wdn_ref, wdn_stage, wdn_sem, math.sqrt(0.5) / ALPHA)

    tm = FFN_SUB_TILE
    sub = V7X_SUBLANES
    vrows = tm // sub
    pitch = vrows + sub
    n_slabs = D_MODEL // V7X_LANES
    seq_start = (pl.program_id(0) % tiles_per_seq) == 0
    sublane = lax.broadcasted_iota(jnp.int32, (sub, 2 * D_FF), 0)
    old = jnp.where(seq_start, 0.0, carry_ref[...])

    n_sub_tiles = x_ref.shape[0] // tm
    xbs = []
    for t in range(n_sub_tiles):
        x = x_ref[t * tm:(t + 1) * tm, :]
        for l in range(n_slabs):
            for s in range(sub):
                xs_ref[t, l, s * pitch:s * pitch + vrows, :] = (
                    x[s * vrows:(s + 1) * vrows, l * V7X_LANES:(l + 1) * V7X_LANES])
        xbs.append(jnp.concatenate(
            [jnp.concatenate([xs_ref[t, l, pl.ds(j, sub, stride=pitch), :]
                              for j in range(vrows)], axis=0)
             for l in range(n_slabs)], axis=1).astype(BF16))

    for t in range(n_sub_tiles):
        x = x_ref[t * tm:(t + 1) * tm, :]
        h = jnp.dot(xbs[t], wup_ref[...], preferred_element_type=F32)
        wrap1 = jnp.where(sublane == 0, pltpu.roll(old[sub:], 1, 0),
                          pltpu.roll(h[tm - sub:tm], 1, 0))
        wrap2 = jnp.where(sublane == 0, pltpu.roll(old[:sub], 1, 0),
                          pltpu.roll(h[tm - 2 * sub:tm - sub], 1, 0))
        old = h[tm - 2 * sub:tm]
        back1 = jnp.concatenate([wrap1, h[:tm - sub]], axis=0)
        back2 = jnp.concatenate([wrap2, wrap1, h[:tm - 2 * sub]], axis=0)
        hc = back2 * cw_ref[0:1] + back1 * cw_ref[1:2] + h * cw_ref[2:3] + cb_ref[...]
        a, gate = hc[:, :D_FF], hc[:, D_FF:]
        act = (a * (1.0 + lax.erf(a)) * gate).astype(BF16)
        vp = vrows // FFN_DOWN_PARTS
        for p in range(FFN_DOWN_PARTS):
            rows = slice(p * vp * sub, (p + 1) * vp * sub)
            f_perm = jnp.dot(act[rows], wdn_ref[...], preferred_element_type=F32)
            for l in range(n_slabs):
                ys_ref[t, l, rows, :] = f_perm[:, l * V7X_LANES:(l + 1) * V7X_LANES]
            for s in range(sub):
                f = jnp.concatenate(
                    [ys_ref[t, l, pl.ds(p * vp * sub + s, vp, stride=sub), :]
                     for l in range(n_slabs)], axis=1)
                times = slice(s * vrows + p * vp, s * vrows + (p + 1) * vp)
                o_ref[t * tm + times.start:t * tm + times.stop, :] = _residual_norm(
                    x[times], f, g_ref[...], b_ref[...])
    carry_ref[...] = old


def _conv_ffn_layer(x2, seq_len, layer, w_up, conv_w, conv_b, w_down, g, b):
    n = x2.shape[0]
    tm = FFN_SUB_TILE * FFN_SUB_TILES
    n_slabs = D_MODEL // V7X_LANES
    pitch = FFN_SUB_TILE // V7X_SUBLANES + V7X_SUBLANES
    half_scale = jnp.concatenate([jnp.full((D_FF,), math.sqrt(0.5), F32), jnp.ones((D_FF,), F32)])
    return pl.pallas_call(
        functools.partial(_ffn_kernel, tiles_per_seq=seq_len // tm, layer=layer),
        grid=(n // tm,),
        in_specs=[
            pl.BlockSpec((tm, D_MODEL), lambda i: (i, 0)),
            _HBM,
            _resident((CONV_WIDTH, 2 * D_FF)),
            _resident((1, 2 * D_FF)),
            _HBM,
            _resident((1, D_MODEL)),
            _resident((1, D_MODEL)),
        ],
        out_specs=pl.BlockSpec((tm, D_MODEL), lambda i: (i, 0)),
        out_shape=jax.ShapeDtypeStruct((n, D_MODEL), F32),
        scratch_shapes=([
            pltpu.VMEM((FFN_SUB_TILES, n_slabs, V7X_SUBLANES * pitch, V7X_LANES), F32),
            pltpu.VMEM((FFN_SUB_TILES, n_slabs, FFN_SUB_TILE, V7X_LANES), F32),
            pltpu.VMEM((2 * V7X_SUBLANES, 2 * D_FF), F32)]
            + _stage_scratch(D_MODEL, 2 * D_FF)
            + _stage_scratch(D_FF, D_MODEL)),
        compiler_params=_params("arbitrary"),
        name="conv_ffn",
    )(x2, w_up, conv_w * half_scale, (conv_b * half_scale).reshape(1, -1),
      w_down, g.reshape(1, -1), b.reshape(1, -1))


def _regroup_proj_kernel(x_ref, w_ref, *rest, plan):
    n_out = len({o for routes in plan for (o, _, _) in routes})
    out_refs, res_ref = rest[:n_out], rest[n_out]
    tm = x_ref.shape[0]
    cb = V7X_MXU_DIM
    rc = ROW_TILE
    xbs = [x_ref[r * rc:(r + 1) * rc, :].astype(BF16) for r in range(tm // rc)]

    def project(c):
        slot = c % res_ref.shape[0]
        for r, xb in enumerate(xbs):
            res = jnp.dot(xb, w_ref[:, c * cb:(c + 1) * cb], preferred_element_type=F32)
            for l in range(cb // V7X_LANES):
                res_ref[slot, l, r * rc:(r + 1) * rc, :] = res[:, l * V7X_LANES:(l + 1) * V7X_LANES]
            for (o, oc, d) in plan[c]:
                if d == 1:
                    out_refs[o][r * rc:(r + 1) * rc, oc * cb:(oc + 1) * cb] = (
                        res.astype(out_refs[o].dtype))

    def regroup(c):
        slot = c % res_ref.shape[0]
        for (o, oc, d) in plan[c]:
            out = out_refs[o]
            if d == 1:
                continue
            span = BLK * d
            for s in range(tm // span):
                for r in range(d):
                    rows = jnp.concatenate(
                        [res_ref[slot, l, pl.ds(s * span + r, BLK, stride=d), :]
                         for l in range(cb // V7X_LANES)], axis=-1)
                    out[s * span + r * BLK:s * span + (r + 1) * BLK,
                        oc * cb:(oc + 1) * cb] = rows.astype(out.dtype)

    project(0)
    for c in range(len(plan)):
        if c + 1 < len(plan):
            project(c + 1)
        regroup(c)


def _regroup_proj(x2, w, plan, out_widths, scale=1.0):
    n = x2.shape[0]
    tm = SPAN_TILE
    return pl.pallas_call(
        functools.partial(_regroup_proj_kernel, plan=plan),
        grid=(n // tm,),
        in_specs=[pl.BlockSpec((tm, D_MODEL), lambda i: (i, 0)),
                  _resident(w.shape)],
        out_specs=[pl.BlockSpec((tm, wd), lambda i: (i, 0)) for wd in out_widths],
        out_shape=[jax.ShapeDtypeStruct((n, wd), BF16) for wd in out_widths],
        scratch_shapes=[pltpu.VMEM((2, V7X_MXU_DIM // V7X_LANES, tm, V7X_LANES), F32)],
        compiler_params=_params("parallel"),
        name="regroup_proj",
    )(x2, (w * scale).astype(BF16))


def _bias_kernel(table_ref, idx_ref, o_ref):
    g = pl.program_id(0)
    idx = idx_ref[0]
    row = lax.broadcasted_iota(jnp.int32, idx.shape, 0)
    col = lax.broadcasted_iota(jnp.int32, idx.shape, 1)
    delta = row + BLK - col
    in_band = jnp.logical_and(delta >= 0, delta <= BLK)
    in_band_cur = jnp.logical_and(in_band, col >= BLK)
    for h in range(KV_HEADS):
        acc = jnp.zeros(idx.shape, F32)
        for bkt in range(REL_BUCKETS):
            acc = jnp.where(idx == bkt,
                            table_ref[bkt * (N_GROUPS * KV_HEADS) + g * KV_HEADS + h], acc)
        acc = acc * LOG2E
        o_ref[0, 0, h] = jnp.where(in_band, acc, NEG)
        o_ref[1, 0, h] = jnp.where(in_band_cur, acc, NEG)


def _t5_bucket(dist):
    n = np.asarray(dist, np.int64)
    max_exact = REL_BUCKETS // 2
    nf = np.maximum(n, 1).astype(np.float64)
    large = max_exact + np.floor(np.log(nf / max_exact) / math.log(REL_MAX_DIST / max_exact)
                                 * (REL_BUCKETS - max_exact)).astype(np.int64)
    large = np.minimum(large, REL_BUCKETS - 1)
    return np.where(n < max_exact, n, large)


def _band_bias(rel_table):
    iq = np.arange(BLK)[:, None]
    ik = np.arange(2 * BLK)[None, :]
    delta = np.clip(iq + BLK - ik, 0, None)
    idx = jnp.asarray(np.stack([_t5_bucket(delta * d) for _, d in DILATED_GROUPS]), jnp.int32)
    return pl.pallas_call(
        _bias_kernel,
        grid=(N_GROUPS,),
        in_specs=[pl.BlockSpec(memory_space=pltpu.SMEM),
                  pl.BlockSpec((1, BLK, 2 * BLK), lambda g: (g, 0, 0))],
        out_specs=pl.BlockSpec((2, 1, KV_HEADS, BLK, 2 * BLK), lambda g: (0, g, 0, 0, 0)),
        out_shape=jax.ShapeDtypeStruct((2, N_GROUPS, KV_HEADS, BLK, 2 * BLK), F32),
        compiler_params=_params("parallel"),
        name="band_bias",
    )(rel_table.reshape(-1), idx)


def _attn_kernel(bias_ref, *refs, tiles_per_seq):
    q_refs = refs[0:N_GROUPS]
    kv_refs = refs[N_GROUPS:5 * N_GROUPS]
    o_ref = refs[5 * N_GROUPS]
    og_ref, lse_ref = refs[5 * N_GROUPS + 1:]
    tm = o_ref.shape[0]
    n_blocks = tm // BLK
    seq_start = (pl.program_id(0) % tiles_per_seq) == 0

    q_lane = lax.broadcasted_iota(jnp.int32, (BLK, V7X_LANES), 1) // HEAD_DIM
    kv_lane = lax.broadcasted_iota(jnp.int32, (2 * BLK, V7X_LANES), 1) // HEAD_DIM
    ones_cols = [jnp.where(kv_lane == i, 1.0, 0.0).astype(BF16)
                 for i in range(HEADS_PER_LANE_TILE)]
    nt = (((1,), (1,)), ((), ()))

    def block(g, blk):
        d = DILATED_GROUPS[g][1]
        q_ref = q_refs[g]
        kc_ref, vc_ref, kp_ref, vp_ref = kv_refs[4 * g:4 * g + 4]
        cur = pl.ds(blk * BLK, BLK)
        if blk < d:
            prev = pl.ds(blk * BLK, BLK)
            kprev_ref, vprev_ref = kp_ref, vp_ref
            variant = seq_start.astype(jnp.int32)
            start = blk
        else:
            prev = pl.ds((blk - d) * BLK, BLK)
            kprev_ref, vprev_ref = kc_ref, vc_ref
            variant = 0
            start = (blk // d) * (BLK * d) + blk % d
        rows = pl.ds(start, BLK, stride=d) if d > 1 else cur
        for l in range(HEAD_COLS // V7X_LANES):
            lanes = slice(l * V7X_LANES, (l + 1) * V7X_LANES)
            q = q_ref[cur, lanes]
            k_both = jnp.concatenate([kprev_ref[prev, lanes], kc_ref[cur, lanes]], axis=0)
            v_both = jnp.concatenate([vprev_ref[prev, lanes], vc_ref[cur, lanes]], axis=0)
            ps, vxs = [], []
            m_tile = None
            for i in range(HEADS_PER_LANE_TILE):
                h = l * HEADS_PER_LANE_TILE + i
                qm = jnp.where(q_lane == i, q, jnp.zeros_like(q))
                s = lax.dot_general(qm, k_both, nt, preferred_element_type=F32)
                s = s + bias_ref[variant, g, h]
                m = jnp.max(s, axis=-1, keepdims=True)
                ps.append(jnp.exp2(s - m).astype(BF16))
                vxs.append(jnp.concatenate(
                    [jnp.where(kv_lane == i, v_both, jnp.zeros_like(v_both)),
                     ones_cols[i]], axis=1))
                mb = jnp.broadcast_to(m, (BLK, V7X_LANES))
                m_tile = mb if m_tile is None else jnp.where(q_lane == i, mb, m_tile)
            acc = jnp.dot(jnp.concatenate(ps, axis=1), jnp.concatenate(vxs, axis=0),
                          preferred_element_type=F32)
            den = acc[:, V7X_LANES:]
            og_ref[g, l, rows, :] = acc[:, :V7X_LANES] / den
            lse_ref[g, l, rows, :] = m_tile + jnp.log2(den)

    def mix_groups(lo, hi):
        for l in range(HEAD_COLS // V7X_LANES):
            lse = [lse_ref[g, l, lo:hi, :] for g in range(N_GROUPS)]
            top = functools.reduce(jnp.maximum, lse)
            e = [jnp.exp2(v - top) for v in lse]
            tot = functools.reduce(lambda a, c: a + c, e)
            mix = functools.reduce(lambda a, c: a + c,
                                   [e[g] * og_ref[g, l, lo:hi, :] for g in range(N_GROUPS)])
            o_ref[lo:hi, l * V7X_LANES:(l + 1) * V7X_LANES] = (mix / tot).astype(o_ref.dtype)

    order = sorted(range(N_GROUPS), key=lambda g: -DILATED_GROUPS[g][1])
    widest, rest = order[0], order[1:]
    assert BLK * DILATED_GROUPS[widest][1] == tm
    for blk in range(n_blocks):
        block(widest, blk)
    mix_span = BLK * DILATED_GROUPS[rest[0]][1] if rest else tm
    for lo in range(0, tm, mix_span):
        for g in rest:
            for blk in range(lo // BLK, (lo + mix_span) // BLK):
                block(g, blk)
        mix_groups(lo, lo + mix_span)


def _dilated_attention(q_groups, kv_groups, bias, seq_len):
    n = q_groups[0].shape[0]
    tm = SPAN_TILE
    hcols = HEAD_COLS
    in_specs = [pl.BlockSpec((2, N_GROUPS, HEADS_PER_STEP, BLK, 2 * BLK),
                             lambda t, hh: (0, 0, hh, 0, 0))]
    in_specs += [pl.BlockSpec((tm, hcols), lambda t, hh: (t, hh)) for _ in range(N_GROUPS)]
    args = [bias] + list(q_groups)
    for (_, d), kv in zip(DILATED_GROUPS, kv_groups):
        span = BLK * d
        per_tile = tm // span
        prev_map_k = lambda t, hh, p=per_tile: (jnp.maximum(t * p - 1, 0), hh)
        prev_map_v = lambda t, hh, p=per_tile: (jnp.maximum(t * p - 1, 0), HEAD_SPLIT + hh)
        in_specs += [
            pl.BlockSpec((tm, hcols), lambda t, hh: (t, hh)),
            pl.BlockSpec((tm, hcols), lambda t, hh: (t, HEAD_SPLIT + hh)),
            pl.BlockSpec((span, hcols), prev_map_k),
            pl.BlockSpec((span, hcols), prev_map_v),
        ]
        args += [kv, kv, kv, kv]
    return pl.pallas_call(
        functools.partial(_attn_kernel, tiles_per_seq=seq_len // tm),
        grid=(n // tm, HEAD_SPLIT),
        in_specs=in_specs,
        out_specs=pl.BlockSpec((tm, hcols), lambda t, hh: (t, hh)),
        out_shape=jax.ShapeDtypeStruct((n, KV_WIDTH), BF16),
        scratch_shapes=[pltpu.VMEM((N_GROUPS, hcols // V7X_LANES, tm, V7X_LANES), F32)] * 2,
        compiler_params=_params("parallel", "parallel"),
        name="dilated_attention",
    )(*args)


def _out_proj_kernel(x_ref, o_ref_in, w_ref, g_ref, b_ref, out_ref):
    h = jnp.dot(o_ref_in[...], w_ref[...], preferred_element_type=F32)
    out_ref[...] = _residual_norm(x_ref[...], h, g_ref[...], b_ref[...])


def _out_proj_layer(x2, o, w_o, g, b):
    n = x2.shape[0]
    tm = OUT_PROJ_TILE
    return pl.pallas_call(
        _out_proj_kernel,
        grid=(n // tm,),
        in_specs=[pl.BlockSpec((tm, D_MODEL), lambda i: (i, 0)),
                  pl.BlockSpec((tm, KV_WIDTH), lambda i: (i, 0)),
                  _resident((KV_WIDTH, D_MODEL)),
                  _resident((1, D_MODEL)),
                  _resident((1, D_MODEL))],
        out_specs=pl.BlockSpec((tm, D_MODEL), lambda i: (i, 0)),
        out_shape=jax.ShapeDtypeStruct((n, D_MODEL), F32),
        compiler_params=_params("parallel"),
        name="attn_out_proj",
    )(x2, o, (w_o * (1.0 / ALPHA)).astype(BF16), g.reshape(1, -1), b.reshape(1, -1))


def kernel(x, a_w_in, a_ln_g, a_ln_b, a_w_s, a_b_s, a_w_out, kv_w, b_w_q, b_w_o,
           rel_table, ffn_w_up, ffn_conv_w, ffn_conv_b, ffn_w_down, ln_g, ln_b):
    B, T, _ = x.shape
    assert T % SPAN_TILE == 0 and T % ROW_TILE == 0
    assert T % (FFN_SUB_TILE * FFN_SUB_TILES) == 0
    x2 = x.reshape(B * T, D_MODEL)
    blocks_per_half = KV_WIDTH // V7X_MXU_DIM
    dils = [d for _, d in DILATED_GROUPS]
    kv_plan = tuple(tuple((o, c, d) for o, d in enumerate(dils))
                    for c in range(2 * blocks_per_half))
    q_plan = tuple(((g, c, dils[g]),) for g in range(N_GROUPS) for c in range(blocks_per_half))
    bias = kv_groups = None
    for i in range(DEPTH):
        if i < N_A_LAYERS:
            x2 = _mixer_a_layer(x2, i, a_w_in, a_ln_g[i], a_ln_b[i], a_w_s[i], a_b_s[i],
                                a_w_out, ln_g[i, 0], ln_b[i, 0])
        else:
            if i == N_A_LAYERS:
                kv_groups = _regroup_proj(x2, kv_w, kv_plan, [2 * KV_WIDTH] * N_GROUPS)
                bias = _band_bias(rel_table)
            j = i - N_A_LAYERS
            q_groups = _regroup_proj(x2, b_w_q[j], q_plan, [KV_WIDTH] * N_GROUPS,
                                     scale=HEAD_DIM ** -0.5 * LOG2E)
            o = _dilated_attention(q_groups, kv_groups, bias, T)
            x2 = _out_proj_layer(x2, o, b_w_o[j], ln_g[i, 0], ln_b[i, 0])
        x2 = _conv_ffn_layer(x2, T, i, ffn_w_up, ffn_conv_w[i], ffn_conv_b[i],
                             ffn_w_down, ln_g[i, 1], ln_b[i, 1])
    return x2.reshape(B, T, D_MODEL)
```

```python
import functools
import math

import numpy as np
import jax
import jax.numpy as jnp
from jax import lax
from jax.experimental import pallas as pl
from jax.experimental.pallas import tpu as pltpu

D_MODEL = 1024
DEPTH = 4
N_A_LAYERS = DEPTH // 2
CHUNK = 128
SGU_WIDTH = 2 * D_MODEL
SGU_GROUPS = 8
SGU_GROUP_WIDTH = SGU_WIDTH // SGU_GROUPS
HEAD_DIM = 64
KV_HEADS = D_MODEL // 128
KV_WIDTH = KV_HEADS * HEAD_DIM
DILATED_GROUPS = ((128, 1), (512, 4), (2048, 16))
N_GROUPS = len(DILATED_GROUPS)
BLK = 128
REL_BUCKETS = 32
REL_MAX_DIST = 2048
D_FF = 2816
CONV_WIDTH = 3
ALPHA = (2 * DEPTH) ** 0.25
LN_EPS = 1e-5
NEG = -1e30

F32 = jnp.float32
BF16 = jnp.bfloat16

V7X_SUBLANES = 8
V7X_LANES = 128
V7X_MXU_DIM = 256
V7X_VMEM_LIMIT_BYTES = 56 * 1024 * 1024

ROW_TILE = 512
OUT_PROJ_TILE = 1024
WEIGHT_STAGE_BYTES = 1024 * 1024
WEIGHT_STAGE_SLOTS = 4
FFN_SUB_TILE = 512
FFN_SUB_TILES = 1
FFN_DOWN_PARTS = 2
REGROUP_STRIDE = 4
SGU_OUT_PARTS = 2
SGU_IN_PARTS = 2
SPAN_TILE = max(w for w, _ in DILATED_GROUPS)
HEAD_SPLIT = 2
HEAD_COLS = KV_WIDTH // HEAD_SPLIT
HEADS_PER_STEP = KV_HEADS // HEAD_SPLIT
HEADS_PER_LANE_TILE = V7X_LANES // HEAD_DIM
LOG2E = math.log2(math.e)

assert all(w // d == BLK for w, d in DILATED_GROUPS)


def _layer_norm(x, g, b, eps=LN_EPS):
    mu = jnp.mean(x, axis=-1, keepdims=True)
    xc = x - mu
    var = jnp.mean(xc * xc, axis=-1, keepdims=True)
    return xc * lax.rsqrt(var + eps) * g + b


def _residual_norm(x, h_scaled, g, b):
    return _layer_norm(x + h_scaled, g, b, eps=LN_EPS / (ALPHA * ALPHA))


def _params(*semantics):
    return pltpu.CompilerParams(dimension_semantics=semantics,
                                vmem_limit_bytes=V7X_VMEM_LIMIT_BYTES)


def _resident(shape):
    zeros = (0,) * len(shape)
    return pl.BlockSpec(shape, lambda *_: zeros, pipeline_mode=pl.Buffered(1))


_HBM = pl.BlockSpec(memory_space=pl.ANY)


def _stage_weight(w_hbm, dst_ref, stage_ref, sem_ref, scale):
    slots, chunk = stage_ref.shape[0], stage_ref.shape[1]
    n_chunks = w_hbm.shape[0] // chunk

    def chunk_copy(k):
        return pltpu.make_async_copy(w_hbm.at[pl.ds(k * chunk, chunk), :],
                                     stage_ref.at[k % slots], sem_ref.at[k % slots])

    for k in range(min(slots - 1, n_chunks)):
        chunk_copy(k).start()
    for k in range(n_chunks):
        if k + slots - 1 < n_chunks:
            chunk_copy(k + slots - 1).start()
        chunk_copy(k).wait()
        w = stage_ref[k % slots]
        if scale != 1.0:
            w = w * scale
        dst_ref[k * chunk:(k + 1) * chunk, :] = w.astype(dst_ref.dtype)


def _stage_scratch(rows, cols):
    chunk = max(c for c in range(V7X_SUBLANES, rows + 1, V7X_SUBLANES)
                if rows % c == 0 and c * cols * 4 <= WEIGHT_STAGE_BYTES)
    return [pltpu.VMEM((rows, cols), BF16),
            pltpu.VMEM((WEIGHT_STAGE_SLOTS, chunk, cols), F32),
            pltpu.SemaphoreType.DMA((WEIGHT_STAGE_SLOTS,))]


def _sgu_kernel(x_ref, win_hbm, lng_ref, lnb_ref, ws_ref, bs_ref, wout_hbm,
                g_ref, b_ref, o_ref, vn_ref, win_ref, win_stage, win_sem,
                wout_ref, wout_stage, wout_sem, *, layer):
    tm = x_ref.shape[0]

    @pl.when(pl.program_id(0) == 0)
    def _():
        _stage_weight(win_hbm.at[layer], win_ref, win_stage, win_sem, math.sqrt(0.5))
        _stage_weight(wout_hbm.at[layer], wout_ref, wout_stage, wout_sem,
                      math.sqrt(0.5) / ALPHA)

    x = x_ref[...]
    xb = x.astype(BF16)
    rv = tm // SGU_IN_PARTS
    for p in range(SGU_IN_PARTS):
        rows = slice(p * rv, (p + 1) * rv)
        zv = jnp.dot(xb[rows], win_ref[:, SGU_WIDTH:], preferred_element_type=F32)
        v = (math.sqrt(0.5) * zv) * (1.0 + lax.erf(zv))
        vn_ref[rows, :] = _layer_norm(v, lng_ref[...], lnb_ref[...]).astype(BF16)
    zu = jnp.dot(xb, win_ref[:, :SGU_WIDTH], preferred_element_type=F32)
    u = zu * (1.0 + lax.erf(zu))

    row = lax.broadcasted_iota(jnp.int32, (CHUNK, CHUNK), 0)
    col = lax.broadcasted_iota(jnp.int32, (CHUNK, CHUNK), 1)
    causal = row >= col
    sv = []
    for g in range(SGU_GROUPS):
        cols = slice(g * SGU_GROUP_WIDTH, (g + 1) * SGU_GROUP_WIDTH)
        ws = jnp.where(causal, ws_ref[g], 0.0).astype(BF16)
        sv.append(jnp.concatenate(
            [jnp.dot(ws, vn_ref[c * CHUNK:(c + 1) * CHUNK, cols], preferred_element_type=F32)
             + bs_ref[:, cols] for c in range(tm // CHUNK)], axis=0))
    y = (u * jnp.concatenate(sv, axis=1)).astype(BF16)
    rp = tm // SGU_OUT_PARTS
    for p in range(SGU_OUT_PARTS):
        rows = slice(p * rp, (p + 1) * rp)
        h = jnp.dot(y[rows], wout_ref[...], preferred_element_type=F32)
        o_ref[rows, :] = _residual_norm(x[rows], h, g_ref[...], b_ref[...])


def _mixer_a_layer(x2, layer, w_in, ln_g, ln_b, w_s, b_s, w_out, g, b):
    n = x2.shape[0]
    tm = ROW_TILE
    return pl.pallas_call(
        functools.partial(_sgu_kernel, layer=layer),
        grid=(n // tm,),
        in_specs=[
            pl.BlockSpec((tm, D_MODEL), lambda i: (i, 0)),
            _HBM,
            _resident((1, SGU_WIDTH)),
            _resident((1, SGU_WIDTH)),
            _resident((SGU_GROUPS, CHUNK, CHUNK)),
            _resident((CHUNK, SGU_WIDTH)),
            _HBM,
            _resident((1, D_MODEL)),
            _resident((1, D_MODEL)),
        ],
        out_specs=pl.BlockSpec((tm, D_MODEL), lambda i: (i, 0)),
        out_shape=jax.ShapeDtypeStruct((n, D_MODEL), F32),
        scratch_shapes=([pltpu.VMEM((tm, SGU_WIDTH), BF16)]
                        + _stage_scratch(D_MODEL, 2 * SGU_WIDTH)
                        + _stage_scratch(SGU_WIDTH, D_MODEL)),
        compiler_params=_params("arbitrary"),
        name="sgu_mixer",
    )(x2, w_in, ln_g.reshape(1, -1), ln_b.reshape(1, -1), w_s,
      jnp.repeat(b_s.T, SGU_GROUP_WIDTH, axis=1), w_out, g.reshape(1, -1), b.reshape(1, -1))


def _ffn_kernel(x_ref, wup_hbm, cw_ref, cb_ref, wdn_hbm, g_ref, b_ref, o_ref,
                xs_ref, ys_ref, carry_ref, wup_ref, wup_stage, wup_sem,
                wdn_ref, wdn_stage, wdn_sem, *, tiles_per_seq, layer):
    @pl.when(pl.program_id(0) == 0)
    def _():
        _stage_weight(wup_hbm.at[layer], wup_ref, wup_stage, wup_sem, 1.0)
        _stage_weight(wdn_hbm.at[layer], wdn_ref, wdn_stage, wdn_sem, math.sqrt(0.5) / ALPHA)

    tm = FFN_SUB_TILE
    sub = V7X_SUBLANES
    vrows = tm // sub
    pitch = vrows + sub
    n_slabs = D_MODEL // V7X_LANES
    seq_start = (pl.program_id(0) % tiles_per_seq) == 0
    sublane = lax.broadcasted_iota(jnp.int32, (sub, 2 * D_FF), 0)
    old = jnp.where(seq_start, 0.0, carry_ref[...])

    n_sub_tiles = x_ref.shape[0] // tm
    xbs = []
    for t in range(n_sub_tiles):
        x = x_ref[t * tm:(t + 1) * tm, :]
        for l in range(n_slabs):
            for s in range(sub):
                xs_ref[t, l, s * pitch:s * pitch + vrows, :] = (
                    x[s * vrows:(s + 1) * vrows, l * V7X_LANES:(l + 1) * V7X_LANES])
        xbs.append(jnp.concatenate(
            [jnp.concatenate([xs_ref[t, l, pl.ds(j, sub, stride=pitch), :]
                              for j in range(vrows)], axis=0)
             for l in range(n_slabs)], axis=1).astype(BF16))

    for t in range(n_sub_tiles):
        x = x_ref[t * tm:(t + 1) * tm, :]
        h = jnp.dot(xbs[t], wup_ref[...], preferred_element_type=F32)
        wrap1 = jnp.where(sublane == 0, pltpu.roll(old[sub:], 1, 0),
                          pltpu.roll(h[tm - sub:tm], 1, 0))
        wrap2 = jnp.where(sublane == 0, pltpu.roll(old[:sub], 1, 0),
                          pltpu.roll(h[tm - 2 * sub:tm - sub], 1, 0))
        old = h[tm - 2 * sub:tm]
        back1 = jnp.concatenate([wrap1, h[:tm - sub]], axis=0)
        back2 = jnp.concatenate([wrap2, wrap1, h[:tm - 2 * sub]], axis=0)
        hc = back2 * cw_ref[0:1] + back1 * cw_ref[1:2] + h * cw_ref[2:3] + cb_ref[...]
        a, gate = hc[:, :D_FF], hc[:, D_FF:]
        act = (a * (1.0 + lax.erf(a)) * gate).astype(BF16)
        vp = vrows // FFN_DOWN_PARTS
        for p in range(FFN_DOWN_PARTS):
            rows = slice(p * vp * sub, (p + 1) * vp * sub)
            f_perm = jnp.dot(act[rows], wdn_ref[...], preferred_element_type=F32)
            for l in range(n_slabs):
                ys_ref[t, l, rows, :] = f_perm[:, l * V7X_LANES:(l + 1) * V7X_LANES]
            for s in range(sub):
                f = jnp.concatenate(
                    [ys_ref[t, l, pl.ds(p * vp * sub + s, vp, stride=sub), :]
                     for l in range(n_slabs)], axis=1)
                times = slice(s * vrows + p * vp, s * vrows + (p + 1) * vp)
                o_ref[t * tm + times.start:t * tm + times.stop, :] = _residual_norm(
                    x[times], f, g_ref[...], b_ref[...])
    carry_ref[...] = old


def _conv_ffn_layer(x2, seq_len, layer, w_up, conv_w, conv_b, w_down, g, b):
    n = x2.shape[0]
    tm = FFN_SUB_TILE * FFN_SUB_TILES
    n_slabs = D_MODEL // V7X_LANES
    pitch = FFN_SUB_TILE // V7X_SUBLANES + V7X_SUBLANES
    half_scale = jnp.concatenate([jnp.full((D_FF,), math.sqrt(0.5), F32), jnp.ones((D_FF,), F32)])
    return pl.pallas_call(
        functools.partial(_ffn_kernel, tiles_per_seq=seq_len // tm, layer=layer),
        grid=(n // tm,),
        in_specs=[
            pl.BlockSpec((tm, D_MODEL), lambda i: (i, 0)),
            _HBM,
            _resident((CONV_WIDTH, 2 * D_FF)),
            _resident((1, 2 * D_FF)),
            _HBM,
            _resident((1, D_MODEL)),
            _resident((1, D_MODEL)),
        ],
        out_specs=pl.BlockSpec((tm, D_MODEL), lambda i: (i, 0)),
        out_shape=jax.ShapeDtypeStruct((n, D_MODEL), F32),
        scratch_shapes=([
            pltpu.VMEM((FFN_SUB_TILES, n_slabs, V7X_SUBLANES * pitch, V7X_LANES), F32),
            pltpu.VMEM((FFN_SUB_TILES, n_slabs, FFN_SUB_TILE, V7X_LANES), F32),
            pltpu.VMEM((2 * V7X_SUBLANES, 2 * D_FF), F32)]
            + _stage_scratch(D_MODEL, 2 * D_FF)
            + _stage_scratch(D_FF, D_MODEL)),
        compiler_params=_params("arbitrary"),
        name="conv_ffn",
    )(x2, w_up, conv_w * half_scale, (conv_b * half_scale).reshape(1, -1),
      w_down, g.reshape(1, -1), b.reshape(1, -1))


def _regroup_proj_kernel(x_ref, w_ref, *rest, plan):
    n_out = len({o for routes in plan for (o, _, _) in routes})
    out_refs, res_ref, mid_ref = rest[:n_out], rest[n_out], rest[n_out + 1]
    tm = x_ref.shape[0]
    cb = V7X_MXU_DIM
    rc = ROW_TILE
    xbs = [x_ref[r * rc:(r + 1) * rc, :].astype(BF16) for r in range(tm // rc)]

    def project(c):
        slot = c % res_ref.shape[0]
        for r, xb in enumerate(xbs):
            res = jnp.dot(xb, w_ref[:, c * cb:(c + 1) * cb], preferred_element_type=F32)
            for l in range(cb // V7X_LANES):
                res_ref[slot, l, r * rc:(r + 1) * rc, :] = res[:, l * V7X_LANES:(l + 1) * V7X_LANES]
            for (o, oc, d) in plan[c]:
                if d == 1:
                    out_refs[o][r * rc:(r + 1) * rc, oc * cb:(oc + 1) * cb] = (
                        res.astype(out_refs[o].dtype))

    def regroup(c):
        slot = c % res_ref.shape[0]
        n_slabs = cb // V7X_LANES
        ds1 = REGROUP_STRIDE
        span1 = BLK * ds1
        direct = [(o, oc) for (o, oc, d) in plan[c] if d == ds1]
        wide = [(o, oc, d) for (o, oc, d) in plan[c] if d > ds1]
        assert all(d % ds1 == 0 for (_, _, d) in wide)
        assert len(direct) + len(wide) + sum(d == 1 for (_, _, d) in plan[c]) == len(plan[c])
        if not direct and not wide:
            return
        for s in range(tm // span1):
            for r in range(ds1):
                parts = [res_ref[slot, l, pl.ds(s * span1 + r, BLK, stride=ds1), :]
                         for l in range(n_slabs)]
                dst = slice(s * span1 + r * BLK, s * span1 + (r + 1) * BLK)
                if wide:
                    for l in range(n_slabs):
                        mid_ref[slot, l, dst, :] = parts[l]
                for (o, oc) in direct:
                    out_refs[o][dst, oc * cb:(oc + 1) * cb] = (
                        jnp.concatenate(parts, axis=-1).astype(out_refs[o].dtype))
        for (o, oc, d) in wide:
            ds2 = d // ds1
            span = BLK * d
            sub_rows = BLK // ds2
            for s in range(tm // span):
                for r2 in range(ds2):
                    for r1 in range(ds1):
                        rows = jnp.concatenate(
                            [jnp.concatenate(
                                [mid_ref[slot, l, pl.ds((s * ds2 + s1) * span1 + r1 * BLK + r2,
                                                        sub_rows, stride=ds2), :]
                                 for s1 in range(ds2)], axis=0)
                             for l in range(n_slabs)], axis=-1)
                        r = r1 + ds1 * r2
                        out_refs[o][s * span + r * BLK:s * span + (r + 1) * BLK,
                                    oc * cb:(oc + 1) * cb] = rows.astype(out_refs[o].dtype)

    project(0)
    for c in range(len(plan)):
        if c + 1 < len(plan):
            project(c + 1)
        regroup(c)


def _regroup_proj(x2, w, plan, out_widths, scale=1.0):
    n = x2.shape[0]
    tm = SPAN_TILE
    return pl.pallas_call(
        functools.partial(_regroup_proj_kernel, plan=plan),
        grid=(n // tm,),
        in_specs=[pl.BlockSpec((tm, D_MODEL), lambda i: (i, 0)),
                  _resident(w.shape)],
        out_specs=[pl.BlockSpec((tm, wd), lambda i: (i, 0)) for wd in out_widths],
        out_shape=[jax.ShapeDtypeStruct((n, wd), BF16) for wd in out_widths],
        scratch_shapes=[pltpu.VMEM((2, V7X_MXU_DIM // V7X_LANES, tm, V7X_LANES), F32)] * 2,
        compiler_params=_params("parallel"),
        name="regroup_proj",
    )(x2, (w * scale).astype(BF16))


def _bias_kernel(table_ref, idx_ref, o_ref):
    g = pl.program_id(0)
    idx = idx_ref[0]
    row = lax.broadcasted_iota(jnp.int32, idx.shape, 0)
    col = lax.broadcasted_iota(jnp.int32, idx.shape, 1)
    delta = row + BLK - col
    in_band = jnp.logical_and(delta >= 0, delta <= BLK)
    in_band_cur = jnp.logical_and(in_band, col >= BLK)
    for h in range(KV_HEADS):
        acc = jnp.zeros(idx.shape, F32)
        for bkt in range(REL_BUCKETS):
            acc = jnp.where(idx == bkt,
                            table_ref[bkt * (N_GROUPS * KV_HEADS) + g * KV_HEADS + h], acc)
        acc = acc * LOG2E
        o_ref[0, 0, h] = jnp.where(in_band, acc, NEG)
        o_ref[1, 0, h] = jnp.where(in_band_cur, acc, NEG)


def _t5_bucket(dist):
    n = np.asarray(dist, np.int64)
    max_exact = REL_BUCKETS // 2
    nf = np.maximum(n, 1).astype(np.float64)
    large = max_exact + np.floor(np.log(nf / max_exact) / math.log(REL_MAX_DIST / max_exact)
                                 * (REL_BUCKETS - max_exact)).astype(np.int64)
    large = np.minimum(large, REL_BUCKETS - 1)
    return np.where(n < max_exact, n, large)


def _band_bias(rel_table):
    iq = np.arange(BLK)[:, None]
    ik = np.arange(2 * BLK)[None, :]
    delta = np.clip(iq + BLK - ik, 0, None)
    idx = jnp.asarray(np.stack([_t5_bucket(delta * d) for _, d in DILATED_GROUPS]), jnp.int32)
    return pl.pallas_call(
        _bias_kernel,
        grid=(N_GROUPS,),
        in_specs=[pl.BlockSpec(memory_space=pltpu.SMEM),
                  pl.BlockSpec((1, BLK, 2 * BLK), lambda g: (g, 0, 0))],
        out_specs=pl.BlockSpec((2, 1, KV_HEADS, BLK, 2 * BLK), lambda g: (0, g, 0, 0, 0)),
        out_shape=jax.ShapeDtypeStruct((2, N_GROUPS, KV_HEADS, BLK, 2 * BLK), F32),
        compiler_params=_params("parallel"),
        name="band_bias",
    )(rel_table.reshape(-1), idx)


def _attn_kernel(bias_ref, *refs, tiles_per_seq):
    q_refs = refs[0:N_GROUPS]
    kv_refs = refs[N_GROUPS:5 * N_GROUPS]
    o_ref = refs[5 * N_GROUPS]
    og_ref, lse_ref = refs[5 * N_GROUPS + 1:]
    tm = o_ref.shape[0]
    n_blocks = tm // BLK
    seq_start = (pl.program_id(0) % tiles_per_seq) == 0

    q_lane = lax.broadcasted_iota(jnp.int32, (BLK, V7X_LANES), 1) // HEAD_DIM
    kv_lane = lax.broadcasted_iota(jnp.int32, (2 * BLK, V7X_LANES), 1) // HEAD_DIM
    ones_cols = [jnp.where(kv_lane == i, 1.0, 0.0).astype(BF16)
                 for i in range(HEADS_PER_LANE_TILE)]
    nt = (((1,), (1,)), ((), ()))

    def block(g, blk):
        d = DILATED_GROUPS[g][1]
        q_ref = q_refs[g]
        kc_ref, vc_ref, kp_ref, vp_ref = kv_refs[4 * g:4 * g + 4]
        cur = pl.ds(blk * BLK, BLK)
        if blk < d:
            prev = pl.ds(blk * BLK, BLK)
            kprev_ref, vprev_ref = kp_ref, vp_ref
            variant = seq_start.astype(jnp.int32)
            start = blk
        else:
            prev = pl.ds((blk - d) * BLK, BLK)
            kprev_ref, vprev_ref = kc_ref, vc_ref
            variant = 0
            start = (blk // d) * (BLK * d) + blk % d
        rows = pl.ds(start, BLK, stride=d) if d > 1 else cur
        for l in range(HEAD_COLS // V7X_LANES):
            lanes = slice(l * V7X_LANES, (l + 1) * V7X_LANES)
            q = q_ref[cur, lanes]
            k_both = jnp.concatenate([kprev_ref[prev, lanes], kc_ref[cur, lanes]], axis=0)
            v_both = jnp.concatenate([vprev_ref[prev, lanes], vc_ref[cur, lanes]], axis=0)
            ps, vxs = [], []
            m_tile = None
            for i in range(HEADS_PER_LANE_TILE):
                h = l * HEADS_PER_LANE_TILE + i
                qm = jnp.where(q_lane == i, q, jnp.zeros_like(q))
                s = lax.dot_general(qm, k_both, nt, preferred_element_type=F32)
                s = s + bias_ref[variant, g, h]
                m = jnp.max(s, axis=-1, keepdims=True)
                ps.append(jnp.exp2(s - m).astype(BF16))
                vxs.append(jnp.concatenate(
                    [jnp.where(kv_lane == i, v_both, jnp.zeros_like(v_both)),
                     ones_cols[i]], axis=1))
                mb = jnp.broadcast_to(m, (BLK, V7X_LANES))
                m_tile = mb if m_tile is None else jnp.where(q_lane == i, mb, m_tile)
            acc = jnp.dot(jnp.concatenate(ps, axis=1), jnp.concatenate(vxs, axis=0),
                          preferred_element_type=F32)
            den = acc[:, V7X_LANES:]
            og_ref[g, l, rows, :] = acc[:, :V7X_LANES] / den
            lse_ref[g, l, rows, :] = m_tile + jnp.log2(den)

    def mix_groups(lo, hi):
        for l in range(HEAD_COLS // V7X_LANES):
            lse = [lse_ref[g, l, lo:hi, :] for g in range(N_GROUPS)]
            top = functools.reduce(jnp.maximum, lse)
            e = [jnp.exp2(v - top) for v in lse]
            tot = functools.reduce(lambda a, c: a + c, e)
            mix = functools.reduce(lambda a, c: a + c,
                                   [e[g] * og_ref[g, l, lo:hi, :] for g in range(N_GROUPS)])
            o_ref[lo:hi, l * V7X_LANES:(l + 1) * V7X_LANES] = (mix / tot).astype(o_ref.dtype)

    order = sorted(range(N_GROUPS), key=lambda g: -DILATED_GROUPS[g][1])
    widest, rest = order[0], order[1:]
    assert BLK * DILATED_GROUPS[widest][1] == tm
    for blk in range(n_blocks):
        block(widest, blk)
    mix_span = BLK * DILATED_GROUPS[rest[0]][1] if rest else tm
    for lo in range(0, tm, mix_span):
        for g in rest:
            for blk in range(lo // BLK, (lo + mix_span) // BLK):
                block(g, blk)
        mix_groups(lo, lo + mix_span)


def _dilated_attention(q_groups, kv_groups, bias, seq_len):
    n = q_groups[0].shape[0]
    tm = SPAN_TILE
    hcols = HEAD_COLS
    in_specs = [pl.BlockSpec((2, N_GROUPS, HEADS_PER_STEP, BLK, 2 * BLK),
                             lambda t, hh: (0, 0, hh, 0, 0))]
    in_specs += [pl.BlockSpec((tm, hcols), lambda t, hh: (t, hh)) for _ in range(N_GROUPS)]
    args = [bias] + list(q_groups)
    for (_, d), kv in zip(DILATED_GROUPS, kv_groups):
        span = BLK * d
        per_tile = tm // span
        prev_map_k = lambda t, hh, p=per_tile: (jnp.maximum(t * p - 1, 0), hh)
        prev_map_v = lambda t, hh, p=per_tile: (jnp.maximum(t * p - 1, 0), HEAD_SPLIT + hh)
        in_specs += [
            pl.BlockSpec((tm, hcols), lambda t, hh: (t, hh)),
            pl.BlockSpec((tm, hcols), lambda t, hh: (t, HEAD_SPLIT + hh)),
            pl.BlockSpec((span, hcols), prev_map_k),
            pl.BlockSpec((span, hcols), prev_map_v),
        ]
        args += [kv, kv, kv, kv]
    return pl.pallas_call(
        functools.partial(_attn_kernel, tiles_per_seq=seq_len // tm),
        grid=(n // tm, HEAD_SPLIT),
        in_specs=in_specs,
        out_specs=pl.BlockSpec((tm, hcols), lambda t, hh: (t, hh)),
        out_shape=jax.ShapeDtypeStruct((n, KV_WIDTH), BF16),
        scratch_shapes=[pltpu.VMEM((N_GROUPS, hcols // V7X_LANES, tm, V7X_LANES), F32)] * 2,
        compiler_params=_params("parallel", "parallel"),
        name="dilated_attention",
    )(*args)


def _out_proj_kernel(x_ref, o_ref_in, w_ref, g_ref, b_ref, out_ref):
    h = jnp.dot(o_ref_in[...], w_ref[...], preferred_element_type=F32)
    out_ref[...] = _residual_norm(x_ref[...], h, g_ref[...], b_ref[...])


def _out_proj_layer(x2, o, w_o, g, b):
    n = x2.shape[0]
    tm = OUT_PROJ_TILE
    return pl.pallas_call(
        _out_proj_kernel,
        grid=(n // tm,),
        in_specs=[pl.BlockSpec((tm, D_MODEL), lambda i: (i, 0)),
                  pl.BlockSpec((tm, KV_WIDTH), lambda i: (i, 0)),
                  _resident((KV_WIDTH, D_MODEL)),
                  _resident((1, D_MODEL)),
                  _resident((1, D_MODEL))],
        out_specs=pl.BlockSpec((tm, D_MODEL), lambda i: (i, 0)),
        out_shape=jax.ShapeDtypeStruct((n, D_MODEL), F32),
        compiler_params=_params("parallel"),
        name="attn_out_proj",
    )(x2, o, (w_o * (1.0 / ALPHA)).astype(BF16), g.reshape(1, -1), b.reshape(1, -1))


def kernel(x, a_w_in, a_ln_g, a_ln_b, a_w_s, a_b_s, a_w_out, kv_w, b_w_q, b_w_o,
           rel_table, ffn_w_up, ffn_conv_w, ffn_conv_b, ffn_w_down, ln_g, ln_b):
    B, T, _ = x.shape
    assert T % SPAN_TILE == 0 and T % ROW_TILE == 0
    assert T % (FFN_SUB_TILE * FFN_SUB_TILES) == 0
    x2 = x.reshape(B * T, D_MODEL)
    blocks_per_half = KV_WIDTH // V7X_MXU_DIM
    dils = [d for _, d in DILATED_GROUPS]
    kv_plan = tuple(tuple((o, c, d) for o, d in enumerate(dils))
                    for c in range(2 * blocks_per_half))
    q_plan = tuple(((g, c, dils[g]),) for g in range(N_GROUPS) for c in range(blocks_per_half))
    bias = kv_groups = None
    for i in range(DEPTH):
        if i < N_A_LAYERS:
            x2 = _mixer_a_layer(x2, i, a_w_in, a_ln_g[i], a_ln_b[i], a_w_s[i], a_b_s[i],
                                a_w_out, ln_g[i, 0], ln_b[i, 0])
        else:
            if i == N_A_LAYERS:
                kv_groups = _regroup_proj(x2, kv_w, kv_plan, [2 * KV_WIDTH] * N_GROUPS)
                bias = _band_bias(rel_table)
            j = i - N_A_LAYERS
            q_groups = _regroup_proj(x2, b_w_q[j], q_plan, [KV_WIDTH] * N_GROUPS,
                                     scale=HEAD_DIM ** -0.5 * LOG2E)
            o = _dilated_attention(q_groups, kv_groups, bias, T)
            x2 = _out_proj_layer(x2, o, b_w_o[j], ln_g[i, 0], ln_b[i, 0])
        x2 = _conv_ffn_layer(x2, T, i, ffn_w_up, ffn_conv_w[i], ffn_conv_b[i],
                             ffn_w_down, ln_g[i, 1], ln_b[i, 1])
    return x2.reshape(B, T, D_MODEL)
```

```python
import functools
import math

import numpy as np
import jax
import jax.numpy as jnp
from jax import lax
from jax.experimental import pallas as pl
from jax.experimental.pallas import tpu as pltpu

D_MODEL = 1024
DEPTH = 4
N_A_LAYERS = DEPTH // 2
CHUNK = 128
SGU_WIDTH = 2 * D_MODEL
SGU_GROUPS = 8
SGU_GROUP_WIDTH = SGU_WIDTH // SGU_GROUPS
HEAD_DIM = 64
KV_HEADS = D_MODEL // 128
KV_WIDTH = KV_HEADS * HEAD_DIM
DILATED_GROUPS = ((128, 1), (512, 4), (2048, 16))
N_GROUPS = len(DILATED_GROUPS)
BLK = 128
REL_BUCKETS = 32
REL_MAX_DIST = 2048
D_FF = 2816
CONV_WIDTH = 3
ALPHA = (2 * DEPTH) ** 0.25
LN_EPS = 1e-5
NEG = -1e30

F32 = jnp.float32
BF16 = jnp.bfloat16

V7X_SUBLANES = 8
V7X_LANES = 128
V7X_MXU_DIM = 256
V7X_VMEM_LIMIT_BYTES = 56 * 1024 * 1024

ROW_TILE = 512
OUT_PROJ_TILE = 2048
WEIGHT_STAGE_BYTES = 1024 * 1024
WEIGHT_STAGE_SLOTS = 4
FFN_DOWN_PARTS = 2
REGROUP_STRIDE = 4
SGU_OUT_PARTS = 2
SGU_IN_PARTS = 2
SPAN_TILE = max(w for w, _ in DILATED_GROUPS)
HEAD_SPLIT = 2
HEAD_COLS = KV_WIDTH // HEAD_SPLIT
HEADS_PER_STEP = KV_HEADS // HEAD_SPLIT
HEADS_PER_LANE_TILE = V7X_LANES // HEAD_DIM
LOG2E = math.log2(math.e)

assert all(w // d == BLK for w, d in DILATED_GROUPS)


def _layer_norm(x, g, b, eps=LN_EPS):
    mu = jnp.mean(x, axis=-1, keepdims=True)
    xc = x - mu
    var = jnp.mean(xc * xc, axis=-1, keepdims=True)
    return xc * lax.rsqrt(var + eps) * g + b


def _residual_norm(x, h_scaled, g, b):
    return _layer_norm(x + h_scaled, g, b, eps=LN_EPS / (ALPHA * ALPHA))


def _params(*semantics):
    return pltpu.CompilerParams(dimension_semantics=semantics,
                                vmem_limit_bytes=V7X_VMEM_LIMIT_BYTES)


def _resident(shape):
    zeros = (0,) * len(shape)
    return pl.BlockSpec(shape, lambda *_: zeros, pipeline_mode=pl.Buffered(1))


_HBM = pl.BlockSpec(memory_space=pl.ANY)


def _stage_weight(w_hbm, dst_ref, stage_ref, sem_ref, scale):
    slots, chunk = stage_ref.shape[0], stage_ref.shape[1]
    n_chunks = w_hbm.shape[0] // chunk

    def chunk_copy(k):
        return pltpu.make_async_copy(w_hbm.at[pl.ds(k * chunk, chunk), :],
                                     stage_ref.at[k % slots], sem_ref.at[k % slots])

    for k in range(min(slots - 1, n_chunks)):
        chunk_copy(k).start()
    for k in range(n_chunks):
        if k + slots - 1 < n_chunks:
            chunk_copy(k + slots - 1).start()
        chunk_copy(k).wait()
        w = stage_ref[k % slots]
        if scale != 1.0:
            w = w * scale
        dst_ref[k * chunk:(k + 1) * chunk, :] = w.astype(dst_ref.dtype)


def _stage_scratch(rows, cols):
    chunk = max(c for c in range(V7X_SUBLANES, rows + 1, V7X_SUBLANES)
                if rows % c == 0 and c * cols * 4 <= WEIGHT_STAGE_BYTES)
    return [pltpu.VMEM((rows, cols), BF16),
            pltpu.VMEM((WEIGHT_STAGE_SLOTS, chunk, cols), F32),
            pltpu.SemaphoreType.DMA((WEIGHT_STAGE_SLOTS,))]


def _sgu_kernel(x_ref, win_hbm, lng_ref, lnb_ref, ws_ref, bs_ref, wout_hbm,
                g_ref, b_ref, o_ref, vn_ref, win_ref, win_stage, win_sem,
                wout_ref, wout_stage, wout_sem, *, layer):
    tm = x_ref.shape[0]

    @pl.when(pl.program_id(0) == 0)
    def _():
        _stage_weight(win_hbm.at[layer], win_ref, win_stage, win_sem, math.sqrt(0.5))
        _stage_weight(wout_hbm.at[layer], wout_ref, wout_stage, wout_sem,
                      math.sqrt(0.5) / ALPHA)

    x = x_ref[...]
    xb = x.astype(BF16)
    rv = tm // SGU_IN_PARTS
    for p in range(SGU_IN_PARTS):
        rows = slice(p * rv, (p + 1) * rv)
        zv = jnp.dot(xb[rows], win_ref[:, SGU_WIDTH:], preferred_element_type=F32)
        v = (math.sqrt(0.5) * zv) * (1.0 + lax.erf(zv))
        vn_ref[rows, :] = _layer_norm(v, lng_ref[...], lnb_ref[...]).astype(BF16)
    zu = jnp.dot(xb, win_ref[:, :SGU_WIDTH], preferred_element_type=F32)
    u = zu * (1.0 + lax.erf(zu))

    row = lax.broadcasted_iota(jnp.int32, (CHUNK, CHUNK), 0)
    col = lax.broadcasted_iota(jnp.int32, (CHUNK, CHUNK), 1)
    causal = row >= col
    sv = []
    for g in range(SGU_GROUPS):
        cols = slice(g * SGU_GROUP_WIDTH, (g + 1) * SGU_GROUP_WIDTH)
        ws = jnp.where(causal, ws_ref[g], 0.0).astype(BF16)
        sv.append(jnp.concatenate(
            [jnp.dot(ws, vn_ref[c * CHUNK:(c + 1) * CHUNK, cols], preferred_element_type=F32)
             + bs_ref[:, cols] for c in range(tm // CHUNK)], axis=0))
    y = (u * jnp.concatenate(sv, axis=1)).astype(BF16)
    rp = tm // SGU_OUT_PARTS
    for p in range(SGU_OUT_PARTS):
        rows = slice(p * rp, (p + 1) * rp)
        h = jnp.dot(y[rows], wout_ref[...], preferred_element_type=F32)
        o_ref[rows, :] = _residual_norm(x[rows], h, g_ref[...], b_ref[...])


def _mixer_a_layer(x2, layer, w_in, ln_g, ln_b, w_s, b_s, w_out, g, b):
    n = x2.shape[0]
    tm = ROW_TILE
    return pl.pallas_call(
        functools.partial(_sgu_kernel, layer=layer),
        grid=(n // tm,),
        in_specs=[
            pl.BlockSpec((tm, D_MODEL), lambda i: (i, 0)),
            _HBM,
            _resident((1, SGU_WIDTH)),
            _resident((1, SGU_WIDTH)),
            _resident((SGU_GROUPS, CHUNK, CHUNK)),
            _resident((CHUNK, SGU_WIDTH)),
            _HBM,
            _resident((1, D_MODEL)),
            _resident((1, D_MODEL)),
        ],
        out_specs=pl.BlockSpec((tm, D_MODEL), lambda i: (i, 0)),
        out_shape=jax.ShapeDtypeStruct((n, D_MODEL), F32),
        scratch_shapes=([pltpu.VMEM((tm, SGU_WIDTH), BF16)]
                        + _stage_scratch(D_MODEL, 2 * SGU_WIDTH)
                        + _stage_scratch(SGU_WIDTH, D_MODEL)),
        compiler_params=_params("arbitrary"),
        name="sgu_mixer",
    )(x2, w_in, ln_g.reshape(1, -1), ln_b.reshape(1, -1), w_s,
      jnp.repeat(b_s.T, SGU_GROUP_WIDTH, axis=1), w_out, g.reshape(1, -1), b.reshape(1, -1))


def _ffn_kernel(x_ref, wup_hbm, cw_ref, cb_ref, wdn_hbm, g_ref, b_ref, o_ref,
                xs_ref, ys_ref, carry_ref, wup_ref, wup_stage, wup_sem,
                wdn_ref, wdn_stage, wdn_sem, *, tiles_per_seq, layer):
    @pl.when(pl.program_id(0) == 0)
    def _():
        _stage_weight(wup_hbm.at[layer], wup_ref, wup_stage, wup_sem, 1.0)
        _stage_weight(wdn_hbm.at[layer], wdn_ref, wdn_stage, wdn_sem, math.sqrt(0.5) / ALPHA)

    tm = x_ref.shape[0]
    sub = V7X_SUBLANES
    vrows = tm // sub
    pitch = vrows + sub
    n_slabs = D_MODEL // V7X_LANES
    seq_start = (pl.program_id(0) % tiles_per_seq) == 0
    sublane = lax.broadcasted_iota(jnp.int32, (sub, 2 * D_FF), 0)

    x = x_ref[...]
    for l in range(n_slabs):
        for s in range(sub):
            xs_ref[l, s * pitch:s * pitch + vrows, :] = (
                x[s * vrows:(s + 1) * vrows, l * V7X_LANES:(l + 1) * V7X_LANES])
    xb = jnp.concatenate(
        [jnp.concatenate([xs_ref[l, pl.ds(j, sub, stride=pitch), :] for j in range(vrows)],
                         axis=0) for l in range(n_slabs)], axis=1).astype(BF16)

    h = jnp.dot(xb, wup_ref[...], preferred_element_type=F32)
    old = jnp.where(seq_start, 0.0, carry_ref[...])
    carry_ref[...] = h[tm - 2 * sub:tm]
    wrap1 = jnp.where(sublane == 0, pltpu.roll(old[sub:], 1, 0),
                      pltpu.roll(h[tm - sub:tm], 1, 0))
    wrap2 = jnp.where(sublane == 0, pltpu.roll(old[:sub], 1, 0),
                      pltpu.roll(h[tm - 2 * sub:tm - sub], 1, 0))
    back1 = jnp.concatenate([wrap1, h[:tm - sub]], axis=0)
    back2 = jnp.concatenate([wrap2, wrap1, h[:tm - 2 * sub]], axis=0)
    hc = back2 * cw_ref[0:1] + back1 * cw_ref[1:2] + h * cw_ref[2:3] + cb_ref[...]
    a, gate = hc[:, :D_FF], hc[:, D_FF:]
    act = (a * (1.0 + lax.erf(a)) * gate).astype(BF16)

    vp = vrows // FFN_DOWN_PARTS
    for v0 in range(0, vrows, vp):
        rows = slice(v0 * sub, (v0 + vp) * sub)
        f_perm = jnp.dot(act[rows], wdn_ref[...], preferred_element_type=F32)
        for l in range(n_slabs):
            ys_ref[l, rows, :] = f_perm[:, l * V7X_LANES:(l + 1) * V7X_LANES]
        for s in range(sub):
            f = jnp.concatenate(
                [ys_ref[l, pl.ds(v0 * sub + s, vp, stride=sub), :] for l in range(n_slabs)],
                axis=1)
            times = slice(s * vrows + v0, s * vrows + v0 + vp)
            o_ref[times, :] = _residual_norm(x[times], f, g_ref[...], b_ref[...])


def _conv_ffn_layer(x2, seq_len, layer, w_up, conv_w, conv_b, w_down, g, b):
    n = x2.shape[0]
    tm = ROW_TILE
    n_slabs = D_MODEL // V7X_LANES
    pitch = tm // V7X_SUBLANES + V7X_SUBLANES
    half_scale = jnp.concatenate([jnp.full((D_FF,), math.sqrt(0.5), F32), jnp.ones((D_FF,), F32)])
    return pl.pallas_call(
        functools.partial(_ffn_kernel, tiles_per_seq=seq_len // tm, layer=layer),
        grid=(n // tm,),
        in_specs=[
            pl.BlockSpec((tm, D_MODEL), lambda i: (i, 0)),
            _HBM,
            _resident((CONV_WIDTH, 2 * D_FF)),
            _resident((1, 2 * D_FF)),
            _HBM,
            _resident((1, D_MODEL)),
            _resident((1, D_MODEL)),
        ],
        out_specs=pl.BlockSpec((tm, D_MODEL), lambda i: (i, 0)),
        out_shape=jax.ShapeDtypeStruct((n, D_MODEL), F32),
        scratch_shapes=([
            pltpu.VMEM((n_slabs, V7X_SUBLANES * pitch, V7X_LANES), F32),
            pltpu.VMEM((n_slabs, tm, V7X_LANES), F32),
            pltpu.VMEM((2 * V7X_SUBLANES, 2 * D_FF), F32)]
            + _stage_scratch(D_MODEL, 2 * D_FF)
            + _stage_scratch(D_FF, D_MODEL)),
        compiler_params=_params("arbitrary"),
        name="conv_ffn",
    )(x2, w_up, conv_w * half_scale, (conv_b * half_scale).reshape(1, -1),
      w_down, g.reshape(1, -1), b.reshape(1, -1))


def _regroup_proj_kernel(x_ref, w_ref, *rest, plan):
    n_out = len({o for routes in plan for (o, _, _) in routes})
    out_refs, res_ref, mid_ref = rest[:n_out], rest[n_out], rest[n_out + 1]
    tm = x_ref.shape[0]
    cb = V7X_MXU_DIM
    rc = ROW_TILE
    xbs = [x_ref[r * rc:(r + 1) * rc, :].astype(BF16) for r in range(tm // rc)]

    def project(c):
        slot = c % res_ref.shape[0]
        for r, xb in enumerate(xbs):
            res = jnp.dot(xb, w_ref[:, c * cb:(c + 1) * cb], preferred_element_type=F32)
            for l in range(cb // V7X_LANES):
                res_ref[slot, l, r * rc:(r + 1) * rc, :] = res[:, l * V7X_LANES:(l + 1) * V7X_LANES]
            for (o, oc, d) in plan[c]:
                if d == 1:
                    out_refs[o][r * rc:(r + 1) * rc, oc * cb:(oc + 1) * cb] = (
                        res.astype(out_refs[o].dtype))

    def regroup(c):
        slot = c % res_ref.shape[0]
        n_slabs = cb // V7X_LANES
        ds1 = REGROUP_STRIDE
        span1 = BLK * ds1
        direct = [(o, oc) for (o, oc, d) in plan[c] if d == ds1]
        wide = [(o, oc, d) for (o, oc, d) in plan[c] if d > ds1]
        assert all(d % ds1 == 0 for (_, _, d) in wide)
        assert len(direct) + len(wide) + sum(d == 1 for (_, _, d) in plan[c]) == len(plan[c])
        if not direct and not wide:
            return
        for s in range(tm // span1):
            for r in range(ds1):
                parts = [res_ref[slot, l, pl.ds(s * span1 + r, BLK, stride=ds1), :]
                         for l in range(n_slabs)]
                dst = slice(s * span1 + r * BLK, s * span1 + (r + 1) * BLK)
                if wide:
                    for l in range(n_slabs):
                        mid_ref[slot, l, dst, :] = parts[l]
                for (o, oc) in direct:
                    out_refs[o][dst, oc * cb:(oc + 1) * cb] = (
                        jnp.concatenate(parts, axis=-1).astype(out_refs[o].dtype))
        for (o, oc, d) in wide:
            ds2 = d // ds1
            span = BLK * d
            sub_rows = BLK // ds2
            for s in range(tm // span):
                for r2 in range(ds2):
                    for r1 in range(ds1):
                        rows = jnp.concatenate(
                            [jnp.concatenate(
                                [mid_ref[slot, l, pl.ds((s * ds2 + s1) * span1 + r1 * BLK + r2,
                                                        sub_rows, stride=ds2), :]
                                 for s1 in range(ds2)], axis=0)
                             for l in range(n_slabs)], axis=-1)
                        r = r1 + ds1 * r2
                        out_refs[o][s * span + r * BLK:s * span + (r + 1) * BLK,
                                    oc * cb:(oc + 1) * cb] = rows.astype(out_refs[o].dtype)

    project(0)
    for c in range(len(plan)):
        if c + 1 < len(plan):
            project(c + 1)
        regroup(c)


def _regroup_proj(x2, w, plan, out_widths, scale=1.0):
    n = x2.shape[0]
    tm = SPAN_TILE
    return pl.pallas_call(
        functools.partial(_regroup_proj_kernel, plan=plan),
        grid=(n // tm,),
        in_specs=[pl.BlockSpec((tm, D_MODEL), lambda i: (i, 0)),
                  _resident(w.shape)],
        out_specs=[pl.BlockSpec((tm, wd), lambda i: (i, 0)) for wd in out_widths],
        out_shape=[jax.ShapeDtypeStruct((n, wd), BF16) for wd in out_widths],
        scratch_shapes=[pltpu.VMEM((2, V7X_MXU_DIM // V7X_LANES, tm, V7X_LANES), F32)] * 2,
        compiler_params=_params("parallel"),
        name="regroup_proj",
    )(x2, (w * scale).astype(BF16))


def _bias_kernel(table_ref, idx_ref, o_ref):
    g = pl.program_id(0)
    idx = idx_ref[0]
    row = lax.broadcasted_iota(jnp.int32, idx.shape, 0)
    col = lax.broadcasted_iota(jnp.int32, idx.shape, 1)
    delta = row + BLK - col
    in_band = jnp.logical_and(delta >= 0, delta <= BLK)
    in_band_cur = jnp.logical_and(in_band, col >= BLK)
    for h in range(KV_HEADS):
        acc = jnp.zeros(idx.shape, F32)
        for bkt in range(REL_BUCKETS):
            acc = jnp.where(idx == bkt,
                            table_ref[bkt * (N_GROUPS * KV_HEADS) + g * KV_HEADS + h], acc)
        acc = acc * LOG2E
        o_ref[0, 0, h] = jnp.where(in_band, acc, NEG)
        o_ref[1, 0, h] = jnp.where(in_band_cur, acc, NEG)


def _t5_bucket(dist):
    n = np.asarray(dist, np.int64)
    max_exact = REL_BUCKETS // 2
    nf = np.maximum(n, 1).astype(np.float64)
    large = max_exact + np.floor(np.log(nf / max_exact) / math.log(REL_MAX_DIST / max_exact)
                                 * (REL_BUCKETS - max_exact)).astype(np.int64)
    large = np.minimum(large, REL_BUCKETS - 1)
    return np.where(n < max_exact, n, large)


def _band_bias(rel_table):
    iq = np.arange(BLK)[:, None]
    ik = np.arange(2 * BLK)[None, :]
    delta = np.clip(iq + BLK - ik, 0, None)
    idx = jnp.asarray(np.stack([_t5_bucket(delta * d) for _, d in DILATED_GROUPS]), jnp.int32)
    return pl.pallas_call(
        _bias_kernel,
        grid=(N_GROUPS,),
        in_specs=[pl.BlockSpec(memory_space=pltpu.SMEM),
                  pl.BlockSpec((1, BLK, 2 * BLK), lambda g: (g, 0, 0))],
        out_specs=pl.BlockSpec((2, 1, KV_HEADS, BLK, 2 * BLK), lambda g: (0, g, 0, 0, 0)),
        out_shape=jax.ShapeDtypeStruct((2, N_GROUPS, KV_HEADS, BLK, 2 * BLK), F32),
        compiler_params=_params("parallel"),
        name="band_bias",
    )(rel_table.reshape(-1), idx)


def _attn_kernel(bias_ref, *refs, tiles_per_seq):
    q_refs = refs[0:N_GROUPS]
    kv_refs = refs[N_GROUPS:5 * N_GROUPS]
    o_ref = refs[5 * N_GROUPS]
    og_ref, lse_ref = refs[5 * N_GROUPS + 1:]
    tm = o_ref.shape[0]
    n_blocks = tm // BLK
    seq_start = (pl.program_id(0) % tiles_per_seq) == 0

    q_lane = lax.broadcasted_iota(jnp.int32, (BLK, V7X_LANES), 1) // HEAD_DIM
    kv_lane = lax.broadcasted_iota(jnp.int32, (2 * BLK, V7X_LANES), 1) // HEAD_DIM
    ones_cols = [jnp.where(kv_lane == i, 1.0, 0.0).astype(BF16)
                 for i in range(HEADS_PER_LANE_TILE)]
    nt = (((1,), (1,)), ((), ()))

    def block(g, blk):
        d = DILATED_GROUPS[g][1]
        q_ref = q_refs[g]
        kc_ref, vc_ref, kp_ref, vp_ref = kv_refs[4 * g:4 * g + 4]
        cur = pl.ds(blk * BLK, BLK)
        if blk < d:
            prev = pl.ds(blk * BLK, BLK)
            kprev_ref, vprev_ref = kp_ref, vp_ref
            variant = seq_start.astype(jnp.int32)
            start = blk
        else:
            prev = pl.ds((blk - d) * BLK, BLK)
            kprev_ref, vprev_ref = kc_ref, vc_ref
            variant = 0
            start = (blk // d) * (BLK * d) + blk % d
        rows = pl.ds(start, BLK, stride=d) if d > 1 else cur
        for l in range(HEAD_COLS // V7X_LANES):
            lanes = slice(l * V7X_LANES, (l + 1) * V7X_LANES)
            q = q_ref[cur, lanes]
            k_both = jnp.concatenate([kprev_ref[prev, lanes], kc_ref[cur, lanes]], axis=0)
            v_both = jnp.concatenate([vprev_ref[prev, lanes], vc_ref[cur, lanes]], axis=0)
            ps, vxs = [], []
            m_tile = None
            for i in range(HEADS_PER_LANE_TILE):
                h = l * HEADS_PER_LANE_TILE + i
                qm = jnp.where(q_lane == i, q, jnp.zeros_like(q))
                s = lax.dot_general(qm, k_both, nt, preferred_element_type=F32)
                s = s + bias_ref[variant, g, h]
                m = jnp.max(s, axis=-1, keepdims=True)
                ps.append(jnp.exp2(s - m).astype(BF16))
                vxs.append(jnp.concatenate(
                    [jnp.where(kv_lane == i, v_both, jnp.zeros_like(v_both)),
                     ones_cols[i]], axis=1))
                mb = jnp.broadcast_to(m, (BLK, V7X_LANES))
                m_tile = mb if m_tile is None else jnp.where(q_lane == i, mb, m_tile)
            acc = jnp.dot(jnp.concatenate(ps, axis=1), jnp.concatenate(vxs, axis=0),
                          preferred_element_type=F32)
            den = acc[:, V7X_LANES:]
            og_ref[g, l, rows, :] = acc[:, :V7X_LANES] / den
            lse_ref[g, l, rows, :] = m_tile + jnp.log2(den)

    def mix_groups(lo, hi):
        for l in range(HEAD_COLS // V7X_LANES):
            lse = [lse_ref[g, l, lo:hi, :] for g in range(N_GROUPS)]
            top = functools.reduce(jnp.maximum, lse)
            e = [jnp.exp2(v - top) for v in lse]
            tot = functools.reduce(lambda a, c: a + c, e)
            mix = functools.reduce(lambda a, c: a + c,
                                   [e[g] * og_ref[g, l, lo:hi, :] for g in range(N_GROUPS)])
            o_ref[lo:hi, l * V7X_LANES:(l + 1) * V7X_LANES] = (mix / tot).astype(o_ref.dtype)

    order = sorted(range(N_GROUPS), key=lambda g: -DILATED_GROUPS[g][1])
    widest, rest = order[0], order[1:]
    assert BLK * DILATED_GROUPS[widest][1] == tm
    for blk in range(n_blocks):
        block(widest, blk)
    mix_span = BLK * DILATED_GROUPS[rest[0]][1] if rest else tm
    for lo in range(0, tm, mix_span):
        for g in rest:
            for blk in range(lo // BLK, (lo + mix_span) // BLK):
                block(g, blk)
        mix_groups(lo, lo + mix_span)


def _dilated_attention(q_groups, kv_groups, bias, seq_len):
    n = q_groups[0].shape[0]
    tm = SPAN_TILE
    hcols = HEAD_COLS
    in_specs = [pl.BlockSpec((2, N_GROUPS, HEADS_PER_STEP, BLK, 2 * BLK),
                             lambda t, hh: (0, 0, hh, 0, 0))]
    in_specs += [pl.BlockSpec((tm, hcols), lambda t, hh: (t, hh)) for _ in range(N_GROUPS)]
    args = [bias] + list(q_groups)
    for (_, d), kv in zip(DILATED_GROUPS, kv_groups):
        span = BLK * d
        per_tile = tm // span
        prev_map_k = lambda t, hh, p=per_tile: (jnp.maximum(t * p - 1, 0), hh)
        prev_map_v = lambda t, hh, p=per_tile: (jnp.maximum(t * p - 1, 0), HEAD_SPLIT + hh)
        in_specs += [
            pl.BlockSpec((tm, hcols), lambda t, hh: (t, hh)),
            pl.BlockSpec((tm, hcols), lambda t, hh: (t, HEAD_SPLIT + hh)),
            pl.BlockSpec((span, hcols), prev_map_k),
            pl.BlockSpec((span, hcols), prev_map_v),
        ]
        args += [kv, kv, kv, kv]
    return pl.pallas_call(
        functools.partial(_attn_kernel, tiles_per_seq=seq_len // tm),
        grid=(n // tm, HEAD_SPLIT),
        in_specs=in_specs,
        out_specs=pl.BlockSpec((tm, hcols), lambda t, hh: (t, hh)),
        out_shape=jax.ShapeDtypeStruct((n, KV_WIDTH), BF16),
        scratch_shapes=[pltpu.VMEM((N_GROUPS, hcols // V7X_LANES, tm, V7X_LANES), F32)] * 2,
        compiler_params=_params("parallel", "parallel"),
        name="dilated_attention",
    )(*args)


def _out_proj_kernel(x_ref, o_ref_in, w_ref, g_ref, b_ref, out_ref):
    h = jnp.dot(o_ref_in[...], w_ref[...], preferred_element_type=F32)
    out_ref[...] = _residual_norm(x_ref[...], h, g_ref[...], b_ref[...])


def _out_proj_layer(x2, o, w_o, g, b):
    n = x2.shape[0]
    tm = OUT_PROJ_TILE
    return pl.pallas_call(
        _out_proj_kernel,
        grid=(n // tm,),
        in_specs=[pl.BlockSpec((tm, D_MODEL), lambda i: (i, 0)),
                  pl.BlockSpec((tm, KV_WIDTH), lambda i: (i, 0)),
                  _resident((KV_WIDTH, D_MODEL)),
                  _resident((1, D_MODEL)),
                  _resident((1, D_MODEL))],
        out_specs=pl.BlockSpec((tm, D_MODEL), lambda i: (i, 0)),
        out_shape=jax.ShapeDtypeStruct((n, D_MODEL), F32),
        compiler_params=_params("parallel"),
        name="attn_out_proj",
    )(x2, o, (w_o * (1.0 / ALPHA)).astype(BF16), g.reshape(1, -1), b.reshape(1, -1))


def kernel(x, a_w_in, a_ln_g, a_ln_b, a_w_s, a_b_s, a_w_out, kv_w, b_w_q, b_w_o,
           rel_table, ffn_w_up, ffn_conv_w, ffn_conv_b, ffn_w_down, ln_g, ln_b):
    B, T, _ = x.shape
    assert T % SPAN_TILE == 0 and T % ROW_TILE == 0
    x2 = x.reshape(B * T, D_MODEL)
    blocks_per_half = KV_WIDTH // V7X_MXU_DIM
    dils = [d for _, d in DILATED_GROUPS]
    kv_plan = tuple(tuple((o, c, d) for o, d in enumerate(dils))
                    for c in range(2 * blocks_per_half))
    q_plan = tuple(((g, c, dils[g]),) for g in range(N_GROUPS) for c in range(blocks_per_half))
    bias = kv_groups = None
    for i in range(DEPTH):
        if i < N_A_LAYERS:
            x2 = _mixer_a_layer(x2, i, a_w_in, a_ln_g[i], a_ln_b[i], a_w_s[i], a_b_s[i],
                                a_w_out, ln_g[i, 0], ln_b[i, 0])
        else:
            if i == N_A_LAYERS:
                kv_groups = _regroup_proj(x2, kv_w, kv_plan, [2 * KV_WIDTH] * N_GROUPS)
                bias = _band_bias(rel_table)
            j = i - N_A_LAYERS
            q_groups = _regroup_proj(x2, b_w_q[j], q_plan, [KV_WIDTH] * N_GROUPS,
                                     scale=HEAD_DIM ** -0.5 * LOG2E)
            o = _dilated_attention(q_groups, kv_groups, bias, T)
            x2 = _out_proj_layer(x2, o, b_w_o[j], ln_g[i, 0], ln_b[i, 0])
        x2 = _conv_ffn_layer(x2, T, i, ffn_w_up, ffn_conv_w[i], ffn_conv_b[i],
                             ffn_w_down, ln_g[i, 1], ln_b[i, 1])
    return x2.reshape(B, T, D_MODEL)
```

```python
import functools
import math

import numpy as np
import jax
import jax.numpy as jnp
from jax import lax
from jax.experimental import pallas as pl
from jax.experimental.pallas import tpu as pltpu

D_MODEL = 1024
DEPTH = 4
N_A_LAYERS = DEPTH // 2
CHUNK = 128
SGU_WIDTH = 2 * D_MODEL
SGU_GROUPS = 8
SGU_GROUP_WIDTH = SGU_WIDTH // SGU_GROUPS
HEAD_DIM = 64
KV_HEADS = D_MODEL // 128
KV_WIDTH = KV_HEADS * HEAD_DIM
DILATED_GROUPS = ((128, 1), (512, 4), (2048, 16))
N_GROUPS = len(DILATED_GROUPS)
BLK = 128
REL_BUCKETS = 32
REL_MAX_DIST = 2048
D_FF = 2816
CONV_WIDTH = 3
ALPHA = (2 * DEPTH) ** 0.25
LN_EPS = 1e-5
NEG = -1e30

F32 = jnp.float32
BF16 = jnp.bfloat16

V7X_SUBLANES = 8
V7X_LANES = 128
V7X_MXU_DIM = 256
V7X_VMEM_LIMIT_BYTES = 56 * 1024 * 1024

ROW_TILE = 512
WEIGHT_STAGE_BYTES = 1024 * 1024
WEIGHT_STAGE_SLOTS = 4
FFN_DOWN_PARTS = 2
REGROUP_STRIDE = 4
SGU_OUT_PARTS = 2
SGU_IN_PARTS = 2
SPAN_TILE = max(w for w, _ in DILATED_GROUPS)
HEAD_SPLIT = 2
HEAD_COLS = KV_WIDTH // HEAD_SPLIT
HEADS_PER_STEP = KV_HEADS // HEAD_SPLIT
HEADS_PER_LANE_TILE = V7X_LANES // HEAD_DIM
LOG2E = math.log2(math.e)

assert all(w // d == BLK for w, d in DILATED_GROUPS)


def _layer_norm(x, g, b, eps=LN_EPS):
    mu = jnp.mean(x, axis=-1, keepdims=True)
    xc = x - mu
    var = jnp.mean(xc * xc, axis=-1, keepdims=True)
    return xc * lax.rsqrt(var + eps) * g + b


def _residual_norm(x, h_scaled, g, b):
    return _layer_norm(x + h_scaled, g, b, eps=LN_EPS / (ALPHA * ALPHA))


def _params(*semantics):
    return pltpu.CompilerParams(dimension_semantics=semantics,
                                vmem_limit_bytes=V7X_VMEM_LIMIT_BYTES)


def _resident(shape):
    zeros = (0,) * len(shape)
    return pl.BlockSpec(shape, lambda *_: zeros, pipeline_mode=pl.Buffered(1))


_HBM = pl.BlockSpec(memory_space=pl.ANY)


def _stage_weight(w_hbm, dst_ref, stage_ref, sem_ref, scale):
    slots, chunk = stage_ref.shape[0], stage_ref.shape[1]
    n_chunks = w_hbm.shape[0] // chunk

    def chunk_copy(k):
        return pltpu.make_async_copy(w_hbm.at[pl.ds(k * chunk, chunk), :],
                                     stage_ref.at[k % slots], sem_ref.at[k % slots])

    for k in range(min(slots - 1, n_chunks)):
        chunk_copy(k).start()
    for k in range(n_chunks):
        if k + slots - 1 < n_chunks:
            chunk_copy(k + slots - 1).start()
        chunk_copy(k).wait()
        w = stage_ref[k % slots]
        if scale != 1.0:
            w = w * scale
        dst_ref[k * chunk:(k + 1) * chunk, :] = w.astype(dst_ref.dtype)


def _stage_scratch(rows, cols):
    chunk = max(c for c in range(V7X_SUBLANES, rows + 1, V7X_SUBLANES)
                if rows % c == 0 and c * cols * 4 <= WEIGHT_STAGE_BYTES)
    return [pltpu.VMEM((rows, cols), BF16),
            pltpu.VMEM((WEIGHT_STAGE_SLOTS, chunk, cols), F32),
            pltpu.SemaphoreType.DMA((WEIGHT_STAGE_SLOTS,))]


def _sgu_kernel(x_ref, win_hbm, lng_ref, lnb_ref, ws_ref, bs_ref, wout_hbm,
                g_ref, b_ref, o_ref, vn_ref, win_ref, win_stage, win_sem,
                wout_ref, wout_stage, wout_sem, *, layer):
    tm = x_ref.shape[0]

    @pl.when(pl.program_id(0) == 0)
    def _():
        _stage_weight(win_hbm.at[layer], win_ref, win_stage, win_sem, math.sqrt(0.5))
        _stage_weight(wout_hbm.at[layer], wout_ref, wout_stage, wout_sem,
                      math.sqrt(0.5) / ALPHA)

    x = x_ref[...]
    xb = x.astype(BF16)
    rv = tm // SGU_IN_PARTS
    for p in range(SGU_IN_PARTS):
        rows = slice(p * rv, (p + 1) * rv)
        zv = jnp.dot(xb[rows], win_ref[:, SGU_WIDTH:], preferred_element_type=F32)
        v = (math.sqrt(0.5) * zv) * (1.0 + lax.erf(zv))
        vn_ref[rows, :] = _layer_norm(v, lng_ref[...], lnb_ref[...]).astype(BF16)
    zu = jnp.dot(xb, win_ref[:, :SGU_WIDTH], preferred_element_type=F32)
    u = zu * (1.0 + lax.erf(zu))

    row = lax.broadcasted_iota(jnp.int32, (CHUNK, CHUNK), 0)
    col = lax.broadcasted_iota(jnp.int32, (CHUNK, CHUNK), 1)
    causal = row >= col
    sv = []
    for g in range(SGU_GROUPS):
        cols = slice(g * SGU_GROUP_WIDTH, (g + 1) * SGU_GROUP_WIDTH)
        ws = jnp.where(causal, ws_ref[g], 0.0).astype(BF16)
        sv.append(jnp.concatenate(
            [jnp.dot(ws, vn_ref[c * CHUNK:(c + 1) * CHUNK, cols], preferred_element_type=F32)
             + bs_ref[:, cols] for c in range(tm // CHUNK)], axis=0))
    y = (u * jnp.concatenate(sv, axis=1)).astype(BF16)
    rp = tm // SGU_OUT_PARTS
    for p in range(SGU_OUT_PARTS):
        rows = slice(p * rp, (p + 1) * rp)
        h = jnp.dot(y[rows], wout_ref[...], preferred_element_type=F32)
        o_ref[rows, :] = _residual_norm(x[rows], h, g_ref[...], b_ref[...])


def _mixer_a_layer(x2, layer, w_in, ln_g, ln_b, w_s, b_s, w_out, g, b):
    n = x2.shape[0]
    tm = ROW_TILE
    return pl.pallas_call(
        functools.partial(_sgu_kernel, layer=layer),
        grid=(n // tm,),
        in_specs=[
            pl.BlockSpec((tm, D_MODEL), lambda i: (i, 0)),
            _HBM,
            _resident((1, SGU_WIDTH)),
            _resident((1, SGU_WIDTH)),
            _resident((SGU_GROUPS, CHUNK, CHUNK)),
            _resident((CHUNK, SGU_WIDTH)),
            _HBM,
            _resident((1, D_MODEL)),
            _resident((1, D_MODEL)),
        ],
        out_specs=pl.BlockSpec((tm, D_MODEL), lambda i: (i, 0)),
        out_shape=jax.ShapeDtypeStruct((n, D_MODEL), F32),
        scratch_shapes=([pltpu.VMEM((tm, SGU_WIDTH), BF16)]
                        + _stage_scratch(D_MODEL, 2 * SGU_WIDTH)
                        + _stage_scratch(SGU_WIDTH, D_MODEL)),
        compiler_params=_params("arbitrary"),
        name="sgu_mixer",
    )(x2, w_in, ln_g.reshape(1, -1), ln_b.reshape(1, -1), w_s,
      jnp.repeat(b_s.T, SGU_GROUP_WIDTH, axis=1), w_out, g.reshape(1, -1), b.reshape(1, -1))


def _ffn_kernel(x_ref, *refs, tiles_per_seq, layer, attn_proj):
    if attn_proj:
        attn_ref, wo_ref, g0_ref, b0_ref = refs[:4]
        refs = refs[4:]
    (wup_hbm, cw_ref, cb_ref, wdn_hbm, g_ref, b_ref, o_ref, xs_ref, ys_ref, carry_ref,
     wup_ref, wup_stage, wup_sem, wdn_ref, wdn_stage, wdn_sem) = refs

    @pl.when(pl.program_id(0) == 0)
    def _():
        _stage_weight(wup_hbm.at[layer], wup_ref, wup_stage, wup_sem, 1.0)
        _stage_weight(wdn_hbm.at[layer], wdn_ref, wdn_stage, wdn_sem, math.sqrt(0.5) / ALPHA)

    tm = x_ref.shape[0]
    sub = V7X_SUBLANES
    vrows = tm // sub
    pitch = vrows + sub
    n_slabs = D_MODEL // V7X_LANES
    seq_start = (pl.program_id(0) % tiles_per_seq) == 0
    sublane = lax.broadcasted_iota(jnp.int32, (sub, 2 * D_FF), 0)

    x = x_ref[...]
    if attn_proj:
        x = _residual_norm(x, jnp.dot(attn_ref[...], wo_ref[...], preferred_element_type=F32),
                           g0_ref[...], b0_ref[...])
    for l in range(n_slabs):
        for s in range(sub):
            xs_ref[l, s * pitch:s * pitch + vrows, :] = (
                x[s * vrows:(s + 1) * vrows, l * V7X_LANES:(l + 1) * V7X_LANES])
    xb = jnp.concatenate(
        [jnp.concatenate([xs_ref[l, pl.ds(j, sub, stride=pitch), :] for j in range(vrows)],
                         axis=0) for l in range(n_slabs)], axis=1).astype(BF16)

    h = jnp.dot(xb, wup_ref[...], preferred_element_type=F32)
    old = jnp.where(seq_start, 0.0, carry_ref[...])
    carry_ref[...] = h[tm - 2 * sub:tm]
    wrap1 = jnp.where(sublane == 0, pltpu.roll(old[sub:], 1, 0),
                      pltpu.roll(h[tm - sub:tm], 1, 0))
    wrap2 = jnp.where(sublane == 0, pltpu.roll(old[:sub], 1, 0),
                      pltpu.roll(h[tm - 2 * sub:tm - sub], 1, 0))
    back1 = jnp.concatenate([wrap1, h[:tm - sub]], axis=0)
    back2 = jnp.concatenate([wrap2, wrap1, h[:tm - 2 * sub]], axis=0)
    hc = back2 * cw_ref[0:1] + back1 * cw_ref[1:2] + h * cw_ref[2:3] + cb_ref[...]
    a, gate = hc[:, :D_FF], hc[:, D_FF:]
    act = (a * (1.0 + lax.erf(a)) * gate).astype(BF16)

    vp = vrows // FFN_DOWN_PARTS
    for v0 in range(0, vrows, vp):
        rows = slice(v0 * sub, (v0 + vp) * sub)
        f_perm = jnp.dot(act[rows], wdn_ref[...], preferred_element_type=F32)
        for l in range(n_slabs):
            ys_ref[l, rows, :] = f_perm[:, l * V7X_LANES:(l + 1) * V7X_LANES]
        for s in range(sub):
            f = jnp.concatenate(
                [ys_ref[l, pl.ds(v0 * sub + s, vp, stride=sub), :] for l in range(n_slabs)],
                axis=1)
            times = slice(s * vrows + v0, s * vrows + v0 + vp)
            o_ref[times, :] = _residual_norm(x[times], f, g_ref[...], b_ref[...])


def _conv_ffn_layer(x2, seq_len, layer, w_up, conv_w, conv_b, w_down, g, b, attn=None):
    n = x2.shape[0]
    tm = ROW_TILE
    n_slabs = D_MODEL // V7X_LANES
    pitch = tm // V7X_SUBLANES + V7X_SUBLANES
    half_scale = jnp.concatenate([jnp.full((D_FF,), math.sqrt(0.5), F32), jnp.ones((D_FF,), F32)])
    attn_specs, attn_args = [], []
    if attn is not None:
        o, w_o, g0, b0 = attn
        attn_specs = [pl.BlockSpec((tm, KV_WIDTH), lambda i: (i, 0)),
                      _resident((KV_WIDTH, D_MODEL)),
                      _resident((1, D_MODEL)),
                      _resident((1, D_MODEL))]
        attn_args = [o, (w_o * (1.0 / ALPHA)).astype(BF16), g0.reshape(1, -1), b0.reshape(1, -1)]
    return pl.pallas_call(
        functools.partial(_ffn_kernel, tiles_per_seq=seq_len // tm, layer=layer,
                          attn_proj=attn is not None),
        grid=(n // tm,),
        in_specs=[
            pl.BlockSpec((tm, D_MODEL), lambda i: (i, 0)),
            *attn_specs,
            _HBM,
            _resident((CONV_WIDTH, 2 * D_FF)),
            _resident((1, 2 * D_FF)),
            _HBM,
            _resident((1, D_MODEL)),
            _resident((1, D_MODEL)),
        ],
        out_specs=pl.BlockSpec((tm, D_MODEL), lambda i: (i, 0)),
        out_shape=jax.ShapeDtypeStruct((n, D_MODEL), F32),
        scratch_shapes=([
            pltpu.VMEM((n_slabs, V7X_SUBLANES * pitch, V7X_LANES), F32),
            pltpu.VMEM((n_slabs, tm, V7X_LANES), F32),
            pltpu.VMEM((2 * V7X_SUBLANES, 2 * D_FF), F32)]
            + _stage_scratch(D_MODEL, 2 * D_FF)
            + _stage_scratch(D_FF, D_MODEL)),
        compiler_params=_params("arbitrary"),
        name="conv_ffn",
    )(x2, *attn_args, w_up, conv_w * half_scale, (conv_b * half_scale).reshape(1, -1),
      w_down, g.reshape(1, -1), b.reshape(1, -1))


def _regroup_proj_kernel(x_ref, w_ref, *rest, plan):
    n_out = len({o for routes in plan for (o, _, _) in routes})
    out_refs, res_ref, mid_ref = rest[:n_out], rest[n_out], rest[n_out + 1]
    tm = x_ref.shape[0]
    cb = V7X_MXU_DIM
    rc = ROW_TILE
    xbs = [x_ref[r * rc:(r + 1) * rc, :].astype(BF16) for r in range(tm // rc)]

    def project(c):
        slot = c % res_ref.shape[0]
        for r, xb in enumerate(xbs):
            res = jnp.dot(xb, w_ref[:, c * cb:(c + 1) * cb], preferred_element_type=F32)
            for l in range(cb // V7X_LANES):
                res_ref[slot, l, r * rc:(r + 1) * rc, :] = res[:, l * V7X_LANES:(l + 1) * V7X_LANES]
            for (o, oc, d) in plan[c]:
                if d == 1:
                    out_refs[o][r * rc:(r + 1) * rc, oc * cb:(oc + 1) * cb] = (
                        res.astype(out_refs[o].dtype))

    def regroup(c):
        slot = c % res_ref.shape[0]
        n_slabs = cb // V7X_LANES
        ds1 = REGROUP_STRIDE
        span1 = BLK * ds1
        direct = [(o, oc) for (o, oc, d) in plan[c] if d == ds1]
        wide = [(o, oc, d) for (o, oc, d) in plan[c] if d > ds1]
        assert all(d % ds1 == 0 for (_, _, d) in wide)
        assert len(direct) + len(wide) + sum(d == 1 for (_, _, d) in plan[c]) == len(plan[c])
        if not direct and not wide:
            return
        for s in range(tm // span1):
            for r in range(ds1):
                parts = [res_ref[slot, l, pl.ds(s * span1 + r, BLK, stride=ds1), :]
                         for l in range(n_slabs)]
                dst = slice(s * span1 + r * BLK, s * span1 + (r + 1) * BLK)
                if wide:
                    for l in range(n_slabs):
                        mid_ref[slot, l, dst, :] = parts[l]
                for (o, oc) in direct:
                    out_refs[o][dst, oc * cb:(oc + 1) * cb] = (
                        jnp.concatenate(parts, axis=-1).astype(out_refs[o].dtype))
        for (o, oc, d) in wide:
            ds2 = d // ds1
            span = BLK * d
            sub_rows = BLK // ds2
            for s in range(tm // span):
                for r2 in range(ds2):
                    for r1 in range(ds1):
                        rows = jnp.concatenate(
                            [jnp.concatenate(
                                [mid_ref[slot, l, pl.ds((s * ds2 + s1) * span1 + r1 * BLK + r2,
                                                        sub_rows, stride=ds2), :]
                                 for s1 in range(ds2)], axis=0)
                             for l in range(n_slabs)], axis=-1)
                        r = r1 + ds1 * r2
                        out_refs[o][s * span + r * BLK:s * span + (r + 1) * BLK,
                                    oc * cb:(oc + 1) * cb] = rows.astype(out_refs[o].dtype)

    project(0)
    for c in range(len(plan)):
        if c + 1 < len(plan):
            project(c + 1)
        regroup(c)


def _regroup_proj(x2, w, plan, out_widths, scale=1.0):
    n = x2.shape[0]
    tm = SPAN_TILE
    return pl.pallas_call(
        functools.partial(_regroup_proj_kernel, plan=plan),
        grid=(n // tm,),
        in_specs=[pl.BlockSpec((tm, D_MODEL), lambda i: (i, 0)),
                  _resident(w.shape)],
        out_specs=[pl.BlockSpec((tm, wd), lambda i: (i, 0)) for wd in out_widths],
        out_shape=[jax.ShapeDtypeStruct((n, wd), BF16) for wd in out_widths],
        scratch_shapes=[pltpu.VMEM((2, V7X_MXU_DIM // V7X_LANES, tm, V7X_LANES), F32)] * 2,
        compiler_params=_params("parallel"),
        name="regroup_proj",
    )(x2, (w * scale).astype(BF16))


def _bias_kernel(table_ref, idx_ref, o_ref):
    g = pl.program_id(0)
    idx = idx_ref[0]
    row = lax.broadcasted_iota(jnp.int32, idx.shape, 0)
    col = lax.broadcasted_iota(jnp.int32, idx.shape, 1)
    delta = row + BLK - col
    in_band = jnp.logical_and(delta >= 0, delta <= BLK)
    in_band_cur = jnp.logical_and(in_band, col >= BLK)
    for h in range(KV_HEADS):
        acc = jnp.zeros(idx.shape, F32)
        for bkt in range(REL_BUCKETS):
            acc = jnp.where(idx == bkt,
                            table_ref[bkt * (N_GROUPS * KV_HEADS) + g * KV_HEADS + h], acc)
        acc = acc * LOG2E
        o_ref[0, 0, h] = jnp.where(in_band, acc, NEG)
        o_ref[1, 0, h] = jnp.where(in_band_cur, acc, NEG)


def _t5_bucket(dist):
    n = np.asarray(dist, np.int64)
    max_exact = REL_BUCKETS // 2
    nf = np.maximum(n, 1).astype(np.float64)
    large = max_exact + np.floor(np.log(nf / max_exact) / math.log(REL_MAX_DIST / max_exact)
                                 * (REL_BUCKETS - max_exact)).astype(np.int64)
    large = np.minimum(large, REL_BUCKETS - 1)
    return np.where(n < max_exact, n, large)


def _band_bias(rel_table):
    iq = np.arange(BLK)[:, None]
    ik = np.arange(2 * BLK)[None, :]
    delta = np.clip(iq + BLK - ik, 0, None)
    idx = jnp.asarray(np.stack([_t5_bucket(delta * d) for _, d in DILATED_GROUPS]), jnp.int32)
    return pl.pallas_call(
        _bias_kernel,
        grid=(N_GROUPS,),
        in_specs=[pl.BlockSpec(memory_space=pltpu.SMEM),
                  pl.BlockSpec((1, BLK, 2 * BLK), lambda g: (g, 0, 0))],
        out_specs=pl.BlockSpec((2, 1, KV_HEADS, BLK, 2 * BLK), lambda g: (0, g, 0, 0, 0)),
        out_shape=jax.ShapeDtypeStruct((2, N_GROUPS, KV_HEADS, BLK, 2 * BLK), F32),
        compiler_params=_params("parallel"),
        name="band_bias",
    )(rel_table.reshape(-1), idx)


def _attn_kernel(bias_ref, *refs, tiles_per_seq):
    q_refs = refs[0:N_GROUPS]
    kv_refs = refs[N_GROUPS:5 * N_GROUPS]
    o_ref = refs[5 * N_GROUPS]
    og_ref, lse_ref = refs[5 * N_GROUPS + 1:]
    tm = o_ref.shape[0]
    n_blocks = tm // BLK
    seq_start = (pl.program_id(0) % tiles_per_seq) == 0

    q_lane = lax.broadcasted_iota(jnp.int32, (BLK, V7X_LANES), 1) // HEAD_DIM
    kv_lane = lax.broadcasted_iota(jnp.int32, (2 * BLK, V7X_LANES), 1) // HEAD_DIM
    ones_cols = [jnp.where(kv_lane == i, 1.0, 0.0).astype(BF16)
                 for i in range(HEADS_PER_LANE_TILE)]
    nt = (((1,), (1,)), ((), ()))

    def block(g, blk):
        d = DILATED_GROUPS[g][1]
        q_ref = q_refs[g]
        kc_ref, vc_ref, kp_ref, vp_ref = kv_refs[4 * g:4 * g + 4]
        cur = pl.ds(blk * BLK, BLK)
        if blk < d:
            prev = pl.ds(blk * BLK, BLK)
            kprev_ref, vprev_ref = kp_ref, vp_ref
            variant = seq_start.astype(jnp.int32)
            start = blk
        else:
            prev = pl.ds((blk - d) * BLK, BLK)
            kprev_ref, vprev_ref = kc_ref, vc_ref
            variant = 0
            start = (blk // d) * (BLK * d) + blk % d
        rows = pl.ds(start, BLK, stride=d) if d > 1 else cur
        for l in range(HEAD_COLS // V7X_LANES):
            lanes = slice(l * V7X_LANES, (l + 1) * V7X_LANES)
            q = q_ref[cur, lanes]
            k_both = jnp.concatenate([kprev_ref[prev, lanes], kc_ref[cur, lanes]], axis=0)
            v_both = jnp.concatenate([vprev_ref[prev, lanes], vc_ref[cur, lanes]], axis=0)
            ps, vxs = [], []
            m_tile = None
            for i in range(HEADS_PER_LANE_TILE):
                h = l * HEADS_PER_LANE_TILE + i
                qm = jnp.where(q_lane == i, q, jnp.zeros_like(q))
                s = lax.dot_general(qm, k_both, nt, preferred_element_type=F32)
                s = s + bias_ref[variant, g, h]
                m = jnp.max(s, axis=-1, keepdims=True)
                ps.append(jnp.exp2(s - m).astype(BF16))
                vxs.append(jnp.concatenate(
                    [jnp.where(kv_lane == i, v_both, jnp.zeros_like(v_both)),
                     ones_cols[i]], axis=1))
                mb = jnp.broadcast_to(m, (BLK, V7X_LANES))
                m_tile = mb if m_tile is None else jnp.where(q_lane == i, mb, m_tile)
            acc = jnp.dot(jnp.concatenate(ps, axis=1), jnp.concatenate(vxs, axis=0),
                          preferred_element_type=F32)
            den = acc[:, V7X_LANES:]
            og_ref[g, l, rows, :] = acc[:, :V7X_LANES] / den
            lse_ref[g, l, rows, :] = m_tile + jnp.log2(den)

    def mix_groups(lo, hi):
        for l in range(HEAD_COLS // V7X_LANES):
            lse = [lse_ref[g, l, lo:hi, :] for g in range(N_GROUPS)]
            top = functools.reduce(jnp.maximum, lse)
            e = [jnp.exp2(v - top) for v in lse]
            tot = functools.reduce(lambda a, c: a + c, e)
            mix = functools.reduce(lambda a, c: a + c,
                                   [e[g] * og_ref[g, l, lo:hi, :] for g in range(N_GROUPS)])
            o_ref[lo:hi, l * V7X_LANES:(l + 1) * V7X_LANES] = (mix / tot).astype(o_ref.dtype)

    order = sorted(range(N_GROUPS), key=lambda g: -DILATED_GROUPS[g][1])
    widest, rest = order[0], order[1:]
    assert BLK * DILATED_GROUPS[widest][1] == tm
    for blk in range(n_blocks):
        block(widest, blk)
    mix_span = BLK * DILATED_GROUPS[rest[0]][1] if rest else tm
    for lo in range(0, tm, mix_span):
        for g in rest:
            for blk in range(lo // BLK, (lo + mix_span) // BLK):
                block(g, blk)
        mix_groups(lo, lo + mix_span)


def _dilated_attention(q_groups, kv_groups, bias, seq_len):
    n = q_groups[0].shape[0]
    tm = SPAN_TILE
    hcols = HEAD_COLS
    in_specs = [pl.BlockSpec((2, N_GROUPS, HEADS_PER_STEP, BLK, 2 * BLK),
                             lambda t, hh: (0, 0, hh, 0, 0))]
    in_specs += [pl.BlockSpec((tm, hcols), lambda t, hh: (t, hh)) for _ in range(N_GROUPS)]
    args = [bias] + list(q_groups)
    for (_, d), kv in zip(DILATED_GROUPS, kv_groups):
        span = BLK * d
        per_tile = tm // span
        prev_map_k = lambda t, hh, p=per_tile: (jnp.maximum(t * p - 1, 0), hh)
        prev_map_v = lambda t, hh, p=per_tile: (jnp.maximum(t * p - 1, 0), HEAD_SPLIT + hh)
        in_specs += [
            pl.BlockSpec((tm, hcols), lambda t, hh: (t, hh)),
            pl.BlockSpec((tm, hcols), lambda t, hh: (t, HEAD_SPLIT + hh)),
            pl.BlockSpec((span, hcols), prev_map_k),
            pl.BlockSpec((span, hcols), prev_map_v),
        ]
        args += [kv, kv, kv, kv]
    return pl.pallas_call(
        functools.partial(_attn_kernel, tiles_per_seq=seq_len // tm),
        grid=(n // tm, HEAD_SPLIT),
        in_specs=in_specs,
        out_specs=pl.BlockSpec((tm, hcols), lambda t, hh: (t, hh)),
        out_shape=jax.ShapeDtypeStruct((n, KV_WIDTH), BF16),
        scratch_shapes=[pltpu.VMEM((N_GROUPS, hcols // V7X_LANES, tm, V7X_LANES), F32)] * 2,
        compiler_params=_params("parallel", "parallel"),
        name="dilated_attention",
    )(*args)


def kernel(x, a_w_in, a_ln_g, a_ln_b, a_w_s, a_b_s, a_w_out, kv_w, b_w_q, b_w_o,
           rel_table, ffn_w_up, ffn_conv_w, ffn_conv_b, ffn_w_down, ln_g, ln_b):
    B, T, _ = x.shape
    assert T % SPAN_TILE == 0 and T % ROW_TILE == 0
    x2 = x.reshape(B * T, D_MODEL)
    blocks_per_half = KV_WIDTH // V7X_MXU_DIM
    dils = [d for _, d in DILATED_GROUPS]
    kv_plan = tuple(tuple((o, c, d) for o, d in enumerate(dils))
                    for c in range(2 * blocks_per_half))
    q_plan = tuple(((g, c, dils[g]),) for g in range(N_GROUPS) for c in range(blocks_per_half))
    bias = kv_groups = None
    for i in range(DEPTH):
        attn = None
        if i < N_A_LAYERS:
            x2 = _mixer_a_layer(x2, i, a_w_in, a_ln_g[i], a_ln_b[i], a_w_s[i], a_b_s[i],
                                a_w_out, ln_g[i, 0], ln_b[i, 0])
        else:
            if i == N_A_LAYERS:
                kv_groups = _regroup_proj(x2, kv_w, kv_plan, [2 * KV_WIDTH] * N_GROUPS)
                bias = _band_bias(rel_table)
            j = i - N_A_LAYERS
            q_groups = _regroup_proj(x2, b_w_q[j], q_plan, [KV_WIDTH] * N_GROUPS,
                                     scale=HEAD_DIM ** -0.5 * LOG2E)
            o = _dilated_attention(q_groups, kv_groups, bias, T)
            attn = (o, b_w_o[j], ln_g[i, 0], ln_b[i, 0])
        x2 = _conv_ffn_layer(x2, T, i, ffn_w_up, ffn_conv_w[i], ffn_conv_b[i],
                             ffn_w_down, ln_g[i, 1], ln_b[i, 1], attn=attn)
    return x2.reshape(B, T, D_MODEL)
```

```python
import functools
import math

import numpy as np
import jax
import jax.numpy as jnp
from jax import lax
from jax.experimental import pallas as pl
from jax.experimental.pallas import tpu as pltpu

D_MODEL = 1024
DEPTH = 4
N_A_LAYERS = DEPTH // 2
CHUNK = 128
SGU_WIDTH = 2 * D_MODEL
SGU_GROUPS = 8
SGU_GROUP_WIDTH = SGU_WIDTH // SGU_GROUPS
HEAD_DIM = 64
KV_HEADS = D_MODEL // 128
KV_WIDTH = KV_HEADS * HEAD_DIM
DILATED_GROUPS = ((128, 1), (512, 4), (2048, 16))
N_GROUPS = len(DILATED_GROUPS)
BLK = 128
REL_BUCKETS = 32
REL_MAX_DIST = 2048
D_FF = 2816
CONV_WIDTH = 3
ALPHA = (2 * DEPTH) ** 0.25
LN_EPS = 1e-5
NEG = -1e30

F32 = jnp.float32
BF16 = jnp.bfloat16

V7X_SUBLANES = 8
V7X_LANES = 128
V7X_MXU_DIM = 256
V7X_VMEM_LIMIT_BYTES = 56 * 1024 * 1024

ROW_TILE = 512
WEIGHT_STAGE_BYTES = 1024 * 1024
WEIGHT_STAGE_SLOTS = 4
FFN_DOWN_PARTS = 2
REGROUP_STRIDE = 4
SGU_OUT_PARTS = 2
SGU_IN_PARTS = 2
SPAN_TILE = max(w for w, _ in DILATED_GROUPS)
HEAD_SPLIT = 2
HEAD_COLS = KV_WIDTH // HEAD_SPLIT
HEADS_PER_STEP = KV_HEADS // HEAD_SPLIT
HEADS_PER_LANE_TILE = V7X_LANES // HEAD_DIM
LOG2E = math.log2(math.e)

assert all(w // d == BLK for w, d in DILATED_GROUPS)


def _layer_norm(x, g, b, eps=LN_EPS):
    mu = jnp.mean(x, axis=-1, keepdims=True)
    xc = x - mu
    var = jnp.mean(xc * xc, axis=-1, keepdims=True)
    return xc * lax.rsqrt(var + eps) * g + b


def _residual_norm(x, h_scaled, g, b):
    return _layer_norm(x + h_scaled, g, b, eps=LN_EPS / (ALPHA * ALPHA))


def _params(*semantics):
    return pltpu.CompilerParams(dimension_semantics=semantics,
                                vmem_limit_bytes=V7X_VMEM_LIMIT_BYTES)


def _resident(shape):
    zeros = (0,) * len(shape)
    return pl.BlockSpec(shape, lambda *_: zeros, pipeline_mode=pl.Buffered(1))


_HBM = pl.BlockSpec(memory_space=pl.ANY)


def _stage_weights(*jobs):
    def chunk_copy(job, k):
        w_hbm, _, stage_ref, sem_ref, _ = job
        slots, chunk = stage_ref.shape[0], stage_ref.shape[1]
        return pltpu.make_async_copy(w_hbm.at[pl.ds(k * chunk, chunk), :],
                                     stage_ref.at[k % slots], sem_ref.at[k % slots])

    def n_chunks(job):
        return job[0].shape[0] // job[2].shape[1]

    for job in jobs:
        for k in range(min(job[2].shape[0] - 1, n_chunks(job))):
            chunk_copy(job, k).start()
    for job in jobs:
        _, dst_ref, stage_ref, _, scale = job
        slots, chunk = stage_ref.shape[0], stage_ref.shape[1]
        for k in range(n_chunks(job)):
            if k + slots - 1 < n_chunks(job):
                chunk_copy(job, k + slots - 1).start()
            chunk_copy(job, k).wait()
            w = stage_ref[k % slots]
            if scale != 1.0:
                w = w * scale
            dst_ref[k * chunk:(k + 1) * chunk, :] = w.astype(dst_ref.dtype)


def _stage_scratch(rows, cols):
    chunk = max(c for c in range(V7X_SUBLANES, rows + 1, V7X_SUBLANES)
                if rows % c == 0 and c * cols * 4 <= WEIGHT_STAGE_BYTES)
    return [pltpu.VMEM((rows, cols), BF16),
            pltpu.VMEM((WEIGHT_STAGE_SLOTS, chunk, cols), F32),
            pltpu.SemaphoreType.DMA((WEIGHT_STAGE_SLOTS,))]


def _sgu_kernel(x_ref, win_hbm, lng_ref, lnb_ref, ws_ref, bs_ref, wout_hbm,
                g_ref, b_ref, o_ref, vn_ref, win_ref, win_stage, win_sem,
                wout_ref, wout_stage, wout_sem, *, layer):
    tm = x_ref.shape[0]

    @pl.when(pl.program_id(0) == 0)
    def _():
        _stage_weights(
            (win_hbm.at[layer], win_ref, win_stage, win_sem, math.sqrt(0.5)),
            (wout_hbm.at[layer], wout_ref, wout_stage, wout_sem, math.sqrt(0.5) / ALPHA))

    x = x_ref[...]
    xb = x.astype(BF16)
    rv = tm // SGU_IN_PARTS
    for p in range(SGU_IN_PARTS):
        rows = slice(p * rv, (p + 1) * rv)
        zv = jnp.dot(xb[rows], win_ref[:, SGU_WIDTH:], preferred_element_type=F32)
        v = (math.sqrt(0.5) * zv) * (1.0 + lax.erf(zv))
        vn_ref[rows, :] = _layer_norm(v, lng_ref[...], lnb_ref[...]).astype(BF16)
    zu = jnp.dot(xb, win_ref[:, :SGU_WIDTH], preferred_element_type=F32)
    u = zu * (1.0 + lax.erf(zu))

    row = lax.broadcasted_iota(jnp.int32, (CHUNK, CHUNK), 0)
    col = lax.broadcasted_iota(jnp.int32, (CHUNK, CHUNK), 1)
    causal = row >= col
    sv = []
    for g in range(SGU_GROUPS):
        cols = slice(g * SGU_GROUP_WIDTH, (g + 1) * SGU_GROUP_WIDTH)
        ws = jnp.where(causal, ws_ref[g], 0.0).astype(BF16)
        sv.append(jnp.concatenate(
            [jnp.dot(ws, vn_ref[c * CHUNK:(c + 1) * CHUNK, cols], preferred_element_type=F32)
             + bs_ref[:, cols] for c in range(tm // CHUNK)], axis=0))
    y = (u * jnp.concatenate(sv, axis=1)).astype(BF16)
    rp = tm // SGU_OUT_PARTS
    for p in range(SGU_OUT_PARTS):
        rows = slice(p * rp, (p + 1) * rp)
        h = jnp.dot(y[rows], wout_ref[...], preferred_element_type=F32)
        o_ref[rows, :] = _residual_norm(x[rows], h, g_ref[...], b_ref[...])


def _mixer_a_layer(x2, layer, w_in, ln_g, ln_b, w_s, b_s, w_out, g, b):
    n = x2.shape[0]
    tm = ROW_TILE
    return pl.pallas_call(
        functools.partial(_sgu_kernel, layer=layer),
        grid=(n // tm,),
        in_specs=[
            pl.BlockSpec((tm, D_MODEL), lambda i: (i, 0)),
            _HBM,
            _resident((1, SGU_WIDTH)),
            _resident((1, SGU_WIDTH)),
            _resident((SGU_GROUPS, CHUNK, CHUNK)),
            _resident((CHUNK, SGU_WIDTH)),
            _HBM,
            _resident((1, D_MODEL)),
            _resident((1, D_MODEL)),
        ],
        out_specs=pl.BlockSpec((tm, D_MODEL), lambda i: (i, 0)),
        out_shape=jax.ShapeDtypeStruct((n, D_MODEL), F32),
        scratch_shapes=([pltpu.VMEM((tm, SGU_WIDTH), BF16)]
                        + _stage_scratch(D_MODEL, 2 * SGU_WIDTH)
                        + _stage_scratch(SGU_WIDTH, D_MODEL)),
        compiler_params=_params("arbitrary"),
        name="sgu_mixer",
    )(x2, w_in, ln_g.reshape(1, -1), ln_b.reshape(1, -1), w_s,
      jnp.repeat(b_s.T, SGU_GROUP_WIDTH, axis=1), w_out, g.reshape(1, -1), b.reshape(1, -1))


def _ffn_kernel(x_ref, *refs, tiles_per_seq, layer, attn_proj):
    if attn_proj:
        attn_ref, wo_ref, g0_ref, b0_ref = refs[:4]
        refs = refs[4:]
    (wup_hbm, cw_ref, cb_ref, wdn_hbm, g_ref, b_ref, o_ref, xs_ref, ys_ref, carry_ref,
     wup_ref, wup_stage, wup_sem, wdn_ref, wdn_stage, wdn_sem) = refs

    @pl.when(pl.program_id(0) == 0)
    def _():
        _stage_weights(
            (wup_hbm.at[layer], wup_ref, wup_stage, wup_sem, 1.0),
            (wdn_hbm.at[layer], wdn_ref, wdn_stage, wdn_sem, math.sqrt(0.5) / ALPHA))

    tm = x_ref.shape[0]
    sub = V7X_SUBLANES
    vrows = tm // sub
    pitch = vrows + sub
    n_slabs = D_MODEL // V7X_LANES
    seq_start = (pl.program_id(0) % tiles_per_seq) == 0
    sublane = lax.broadcasted_iota(jnp.int32, (sub, 2 * D_FF), 0)

    x = x_ref[...]
    if attn_proj:
        x = _residual_norm(x, jnp.dot(attn_ref[...], wo_ref[...], preferred_element_type=F32),
                           g0_ref[...], b0_ref[...])
    for l in range(n_slabs):
        for s in range(sub):
            xs_ref[l, s * pitch:s * pitch + vrows, :] = (
                x[s * vrows:(s + 1) * vrows, l * V7X_LANES:(l + 1) * V7X_LANES])
    xb = jnp.concatenate(
        [jnp.concatenate([xs_ref[l, pl.ds(j, sub, stride=pitch), :] for j in range(vrows)],
                         axis=0) for l in range(n_slabs)], axis=1).astype(BF16)

    h = jnp.dot(xb, wup_ref[...], preferred_element_type=F32)
    old = jnp.where(seq_start, 0.0, carry_ref[...])
    carry_ref[...] = h[tm - 2 * sub:tm]
    wrap1 = jnp.where(sublane == 0, pltpu.roll(old[sub:], 1, 0),
                      pltpu.roll(h[tm - sub:tm], 1, 0))
    wrap2 = jnp.where(sublane == 0, pltpu.roll(old[:sub], 1, 0),
                      pltpu.roll(h[tm - 2 * sub:tm - sub], 1, 0))
    back1 = jnp.concatenate([wrap1, h[:tm - sub]], axis=0)
    back2 = jnp.concatenate([wrap2, wrap1, h[:tm - 2 * sub]], axis=0)
    hc = back2 * cw_ref[0:1] + back1 * cw_ref[1:2] + h * cw_ref[2:3] + cb_ref[...]
    a, gate = hc[:, :D_FF], hc[:, D_FF:]
    act = (a * (1.0 + lax.erf(a)) * gate).astype(BF16)

    vp = vrows // FFN_DOWN_PARTS
    for v0 in range(0, vrows, vp):
        rows = slice(v0 * sub, (v0 + vp) * sub)
        f_perm = jnp.dot(act[rows], wdn_ref[...], preferred_element_type=F32)
        for l in range(n_slabs):
            ys_ref[l, rows, :] = f_perm[:, l * V7X_LANES:(l + 1) * V7X_LANES]
        for s in range(sub):
            f = jnp.concatenate(
                [ys_ref[l, pl.ds(v0 * sub + s, vp, stride=sub), :] for l in range(n_slabs)],
                axis=1)
            times = slice(s * vrows + v0, s * vrows + v0 + vp)
            o_ref[times, :] = _residual_norm(x[times], f, g_ref[...], b_ref[...])


def _conv_ffn_layer(x2, seq_len, layer, w_up, conv_w, conv_b, w_down, g, b, attn=None):
    n = x2.shape[0]
    tm = ROW_TILE
    n_slabs = D_MODEL // V7X_LANES
    pitch = tm // V7X_SUBLANES + V7X_SUBLANES
    half_scale = jnp.concatenate([jnp.full((D_FF,), math.sqrt(0.5), F32), jnp.ones((D_FF,), F32)])
    attn_specs, attn_args = [], []
    if attn is not None:
        o, w_o, g0, b0 = attn
        attn_specs = [pl.BlockSpec((tm, KV_WIDTH), lambda i: (i, 0)),
                      _resident((KV_WIDTH, D_MODEL)),
                      _resident((1, D_MODEL)),
                      _resident((1, D_MODEL))]
        attn_args = [o, (w_o * (1.0 / ALPHA)).astype(BF16), g0.reshape(1, -1), b0.reshape(1, -1)]
    return pl.pallas_call(
        functools.partial(_ffn_kernel, tiles_per_seq=seq_len // tm, layer=layer,
                          attn_proj=attn is not None),
        grid=(n // tm,),
        in_specs=[
            pl.BlockSpec((tm, D_MODEL), lambda i: (i, 0)),
            *attn_specs,
            _HBM,
            _resident((CONV_WIDTH, 2 * D_FF)),
            _resident((1, 2 * D_FF)),
            _HBM,
            _resident((1, D_MODEL)),
            _resident((1, D_MODEL)),
        ],
        out_specs=pl.BlockSpec((tm, D_MODEL), lambda i: (i, 0)),
        out_shape=jax.ShapeDtypeStruct((n, D_MODEL), F32),
        scratch_shapes=([
            pltpu.VMEM((n_slabs, V7X_SUBLANES * pitch, V7X_LANES), F32),
            pltpu.VMEM((n_slabs, tm, V7X_LANES), F32),
            pltpu.VMEM((2 * V7X_SUBLANES, 2 * D_FF), F32)]
            + _stage_scratch(D_MODEL, 2 * D_FF)
            + _stage_scratch(D_FF, D_MODEL)),
        compiler_params=_params("arbitrary"),
        name="conv_ffn",
    )(x2, *attn_args, w_up, conv_w * half_scale, (conv_b * half_scale).reshape(1, -1),
      w_down, g.reshape(1, -1), b.reshape(1, -1))


def _regroup_proj_kernel(x_ref, w_ref, *rest, plan):
    n_out = len({o for routes in plan for (o, _, _) in routes})
    out_refs, res_ref, mid_ref = rest[:n_out], rest[n_out], rest[n_out + 1]
    tm = x_ref.shape[0]
    cb = V7X_MXU_DIM
    rc = ROW_TILE
    xbs = [x_ref[r * rc:(r + 1) * rc, :].astype(BF16) for r in range(tm // rc)]

    def project(c):
        slot = c % res_ref.shape[0]
        for r, xb in enumerate(xbs):
            res = jnp.dot(xb, w_ref[:, c * cb:(c + 1) * cb], preferred_element_type=F32)
            for l in range(cb // V7X_LANES):
                res_ref[slot, l, r * rc:(r + 1) * rc, :] = res[:, l * V7X_LANES:(l + 1) * V7X_LANES]
            for (o, oc, d) in plan[c]:
                if d == 1:
                    out_refs[o][r * rc:(r + 1) * rc, oc * cb:(oc + 1) * cb] = (
                        res.astype(out_refs[o].dtype))

    def regroup(c):
        slot = c % res_ref.shape[0]
        n_slabs = cb // V7X_LANES
        ds1 = REGROUP_STRIDE
        span1 = BLK * ds1
        direct = [(o, oc) for (o, oc, d) in plan[c] if d == ds1]
        wide = [(o, oc, d) for (o, oc, d) in plan[c] if d > ds1]
        assert all(d % ds1 == 0 for (_, _, d) in wide)
        assert len(direct) + len(wide) + sum(d == 1 for (_, _, d) in plan[c]) == len(plan[c])
        if not direct and not wide:
            return
        for s in range(tm // span1):
            for r in range(ds1):
                parts = [res_ref[slot, l, pl.ds(s * span1 + r, BLK, stride=ds1), :]
                         for l in range(n_slabs)]
                dst = slice(s * span1 + r * BLK, s * span1 + (r + 1) * BLK)
                if wide:
                    for l in range(n_slabs):
                        mid_ref[slot, l, dst, :] = parts[l]
                for (o, oc) in direct:
                    out_refs[o][dst, oc * cb:(oc + 1) * cb] = (
                        jnp.concatenate(parts, axis=-1).astype(out_refs[o].dtype))
        for (o, oc, d) in wide:
            ds2 = d // ds1
            span = BLK * d
            sub_rows = BLK // ds2
            for s in range(tm // span):
                for r2 in range(ds2):
                    for r1 in range(ds1):
                        rows = jnp.concatenate(
                            [jnp.concatenate(
                                [mid_ref[slot, l, pl.ds((s * ds2 + s1) * span1 + r1 * BLK + r2,
                                                        sub_rows, stride=ds2), :]
                                 for s1 in range(ds2)], axis=0)
                             for l in range(n_slabs)], axis=-1)
                        r = r1 + ds1 * r2
                        out_refs[o][s * span + r * BLK:s * span + (r + 1) * BLK,
                                    oc * cb:(oc + 1) * cb] = rows.astype(out_refs[o].dtype)

    project(0)
    for c in range(len(plan)):
        if c + 1 < len(plan):
            project(c + 1)
        regroup(c)


def _regroup_proj(x2, w, plan, out_widths, scale=1.0):
    n = x2.shape[0]
    tm = SPAN_TILE
    return pl.pallas_call(
        functools.partial(_regroup_proj_kernel, plan=plan),
        grid=(n // tm,),
        in_specs=[pl.BlockSpec((tm, D_MODEL), lambda i: (i, 0)),
                  _resident(w.shape)],
        out_specs=[pl.BlockSpec((tm, wd), lambda i: (i, 0)) for wd in out_widths],
        out_shape=[jax.ShapeDtypeStruct((n, wd), BF16) for wd in out_widths],
        scratch_shapes=[pltpu.VMEM((2, V7X_MXU_DIM // V7X_LANES, tm, V7X_LANES), F32)] * 2,
        compiler_params=_params("parallel"),
        name="regroup_proj",
    )(x2, (w * scale).astype(BF16))


def _bias_kernel(table_ref, idx_ref, o_ref):
    g = pl.program_id(0)
    idx = idx_ref[0]
    row = lax.broadcasted_iota(jnp.int32, idx.shape, 0)
    col = lax.broadcasted_iota(jnp.int32, idx.shape, 1)
    delta = row + BLK - col
    in_band = jnp.logical_and(delta >= 0, delta <= BLK)
    in_band_cur = jnp.logical_and(in_band, col >= BLK)
    for h in range(KV_HEADS):
        acc = jnp.zeros(idx.shape, F32)
        for bkt in range(REL_BUCKETS):
            acc = jnp.where(idx == bkt,
                            table_ref[bkt * (N_GROUPS * KV_HEADS) + g * KV_HEADS + h], acc)
        acc = acc * LOG2E
        o_ref[0, 0, h] = jnp.where(in_band, acc, NEG)
        o_ref[1, 0, h] = jnp.where(in_band_cur, acc, NEG)


def _t5_bucket(dist):
    n = np.asarray(dist, np.int64)
    max_exact = REL_BUCKETS // 2
    nf = np.maximum(n, 1).astype(np.float64)
    large = max_exact + np.floor(np.log(nf / max_exact) / math.log(REL_MAX_DIST / max_exact)
                                 * (REL_BUCKETS - max_exact)).astype(np.int64)
    large = np.minimum(large, REL_BUCKETS - 1)
    return np.where(n < max_exact, n, large)


def _band_bias(rel_table):
    iq = np.arange(BLK)[:, None]
    ik = np.arange(2 * BLK)[None, :]
    delta = np.clip(iq + BLK - ik, 0, None)
    idx = jnp.asarray(np.stack([_t5_bucket(delta * d) for _, d in DILATED_GROUPS]), jnp.int32)
    return pl.pallas_call(
        _bias_kernel,
        grid=(N_GROUPS,),
        in_specs=[pl.BlockSpec(memory_space=pltpu.SMEM),
                  pl.BlockSpec((1, BLK, 2 * BLK), lambda g: (g, 0, 0))],
        out_specs=pl.BlockSpec((2, 1, KV_HEADS, BLK, 2 * BLK), lambda g: (0, g, 0, 0, 0)),
        out_shape=jax.ShapeDtypeStruct((2, N_GROUPS, KV_HEADS, BLK, 2 * BLK), F32),
        compiler_params=_params("parallel"),
        name="band_bias",
    )(rel_table.reshape(-1), idx)


def _attn_kernel(bias_ref, *refs, tiles_per_seq):
    q_refs = refs[0:N_GROUPS]
    kv_refs = refs[N_GROUPS:5 * N_GROUPS]
    o_ref = refs[5 * N_GROUPS]
    og_ref, lse_ref = refs[5 * N_GROUPS + 1:]
    tm = o_ref.shape[0]
    n_blocks = tm // BLK
    seq_start = (pl.program_id(0) % tiles_per_seq) == 0

    q_lane = lax.broadcasted_iota(jnp.int32, (BLK, V7X_LANES), 1) // HEAD_DIM
    kv_lane = lax.broadcasted_iota(jnp.int32, (2 * BLK, V7X_LANES), 1) // HEAD_DIM
    ones_cols = [jnp.where(kv_lane == i, 1.0, 0.0).astype(BF16)
                 for i in range(HEADS_PER_LANE_TILE)]
    nt = (((1,), (1,)), ((), ()))

    def block(g, blk):
        d = DILATED_GROUPS[g][1]
        q_ref = q_refs[g]
        kc_ref, vc_ref, kp_ref, vp_ref = kv_refs[4 * g:4 * g + 4]
        cur = pl.ds(blk * BLK, BLK)
        if blk < d:
            prev = pl.ds(blk * BLK, BLK)
            kprev_ref, vprev_ref = kp_ref, vp_ref
            variant = seq_start.astype(jnp.int32)
            start = blk
        else:
            prev = pl.ds((blk - d) * BLK, BLK)
            kprev_ref, vprev_ref = kc_ref, vc_ref
            variant = 0
            start = (blk // d) * (BLK * d) + blk % d
        rows = pl.ds(start, BLK, stride=d) if d > 1 else cur
        for l in range(HEAD_COLS // V7X_LANES):
            lanes = slice(l * V7X_LANES, (l + 1) * V7X_LANES)
            q = q_ref[cur, lanes]
            k_both = jnp.concatenate([kprev_ref[prev, lanes], kc_ref[cur, lanes]], axis=0)
            v_both = jnp.concatenate([vprev_ref[prev, lanes], vc_ref[cur, lanes]], axis=0)
            ps, vxs = [], []
            m_tile = None
            for i in range(HEADS_PER_LANE_TILE):
                h = l * HEADS_PER_LANE_TILE + i
                qm = jnp.where(q_lane == i, q, jnp.zeros_like(q))
                s = lax.dot_general(qm, k_both, nt, preferred_element_type=F32)
                s = s + bias_ref[variant, g, h]
                m = jnp.max(s, axis=-1, keepdims=True)
                ps.append(jnp.exp2(s - m).astype(BF16))
                vxs.append(jnp.concatenate(
                    [jnp.where(kv_lane == i, v_both, jnp.zeros_like(v_both)),
                     ones_cols[i]], axis=1))
                mb = jnp.broadcast_to(m, (BLK, V7X_LANES))
                m_tile = mb if m_tile is None else jnp.where(q_lane == i, mb, m_tile)
            acc = jnp.dot(jnp.concatenate(ps, axis=1), jnp.concatenate(vxs, axis=0),
                          preferred_element_type=F32)
            den = acc[:, V7X_LANES:]
            og_ref[g, l, rows, :] = acc[:, :V7X_LANES] / den
            lse_ref[g, l, rows, :] = m_tile + jnp.log2(den)

    def mix_groups(lo, hi):
        for l in range(HEAD_COLS // V7X_LANES):
            lse = [lse_ref[g, l, lo:hi, :] for g in range(N_GROUPS)]
            top = functools.reduce(jnp.maximum, lse)
            e = [jnp.exp2(v - top) for v in lse]
            tot = functools.reduce(lambda a, c: a + c, e)
            mix = functools.reduce(lambda a, c: a + c,
                                   [e[g] * og_ref[g, l, lo:hi, :] for g in range(N_GROUPS)])
            o_ref[lo:hi, l * V7X_LANES:(l + 1) * V7X_LANES] = (mix / tot).astype(o_ref.dtype)

    order = sorted(range(N_GROUPS), key=lambda g: -DILATED_GROUPS[g][1])
    widest, rest = order[0], order[1:]
    assert BLK * DILATED_GROUPS[widest][1] == tm
    for blk in range(n_blocks):
        block(widest, blk)
    mix_span = BLK * DILATED_GROUPS[rest[0]][1] if rest else tm
    for lo in range(0, tm, mix_span):
        for g in rest:
            for blk in range(lo // BLK, (lo + mix_span) // BLK):
                block(g, blk)
        mix_groups(lo, lo + mix_span)


def _dilated_attention(q_groups, kv_groups, bias, seq_len):
    n = q_groups[0].shape[0]
    tm = SPAN_TILE
    hcols = HEAD_COLS
    in_specs = [pl.BlockSpec((2, N_GROUPS, HEADS_PER_STEP, BLK, 2 * BLK),
                             lambda t, hh: (0, 0, hh, 0, 0))]
    in_specs += [pl.BlockSpec((tm, hcols), lambda t, hh: (t, hh)) for _ in range(N_GROUPS)]
    args = [bias] + list(q_groups)
    for (_, d), kv in zip(DILATED_GROUPS, kv_groups):
        span = BLK * d
        per_tile = tm // span
        prev_map_k = lambda t, hh, p=per_tile: (jnp.maximum(t * p - 1, 0), hh)
        prev_map_v = lambda t, hh, p=per_tile: (jnp.maximum(t * p - 1, 0), HEAD_SPLIT + hh)
        in_specs += [
            pl.BlockSpec((tm, hcols), lambda t, hh: (t, hh)),
            pl.BlockSpec((tm, hcols), lambda t, hh: (t, HEAD_SPLIT + hh)),
            pl.BlockSpec((span, hcols), prev_map_k),
            pl.BlockSpec((span, hcols), prev_map_v),
        ]
        args += [kv, kv, kv, kv]
    return pl.pallas_call(
        functools.partial(_attn_kernel, tiles_per_seq=seq_len // tm),
        grid=(n // tm, HEAD_SPLIT),
        in_specs=in_specs,
        out_specs=pl.BlockSpec((tm, hcols), lambda t, hh: (t, hh)),
        out_shape=jax.ShapeDtypeStruct((n, KV_WIDTH), BF16),
        scratch_shapes=[pltpu.VMEM((N_GROUPS, hcols // V7X_LANES, tm, V7X_LANES), F32)] * 2,
        compiler_params=_params("parallel", "parallel"),
        name="dilated_attention",
    )(*args)


def kernel(x, a_w_in, a_ln_g, a_ln_b, a_w_s, a_b_s, a_w_out, kv_w, b_w_q, b_w_o,
           rel_table, ffn_w_up, ffn_conv_w, ffn_conv_b, ffn_w_down, ln_g, ln_b):
    B, T, _ = x.shape
    assert T % SPAN_TILE == 0 and T % ROW_TILE == 0
    x2 = x.reshape(B * T, D_MODEL)
    blocks_per_half = KV_WIDTH // V7X_MXU_DIM
    dils = [d for _, d in DILATED_GROUPS]
    kv_plan = tuple(tuple((o, c, d) for o, d in enumerate(dils))
                    for c in range(2 * blocks_per_half))
    q_plan = tuple(((g, c, dils[g]),) for g in range(N_GROUPS) for c in range(blocks_per_half))
    bias = kv_groups = None
    for i in range(DEPTH):
        attn = None
        if i < N_A_LAYERS:
            x2 = _mixer_a_layer(x2, i, a_w_in, a_ln_g[i], a_ln_b[i], a_w_s[i], a_b_s[i],
                                a_w_out, ln_g[i, 0], ln_b[i, 0])
        else:
            if i == N_A_LAYERS:
                kv_groups = _regroup_proj(x2, kv_w, kv_plan, [2 * KV_WIDTH] * N_GROUPS)
                bias = _band_bias(rel_table)
            j = i - N_A_LAYERS
            q_groups = _regroup_proj(x2, b_w_q[j], q_plan, [KV_WIDTH] * N_GROUPS,
                                     scale=HEAD_DIM ** -0.5 * LOG2E)
            o = _dilated_attention(q_groups, kv_groups, bias, T)
            attn = (o, b_w_o[j], ln_g[i, 0], ln_b[i, 0])
        x2 = _conv_ffn_layer(x2, T, i, ffn_w_up, ffn_conv_w[i], ffn_conv_b[i],
                             ffn_w_down, ln_g[i, 1], ln_b[i, 1], attn=attn)
    return x2.reshape(B, T, D_MODEL)
```
